```python
import jax
import jax.numpy as jnp
from jax import lax

D_MODEL = 1024
BATCH = 16
SEQ = 2048
DEPTH = 1

N_POOL_GROUPS = 4
POOL_WINDOWS = (2, 4, 8, 16)
POOL_WIDTH = D_MODEL
POOL_GROUP_DIM = POOL_WIDTH // N_POOL_GROUPS
CONV_WIDTH = D_MODEL
CONV_KSIZE = 3
N_BRANCHES = 2
D_IN = POOL_WIDTH + 3 * CONV_WIDTH + N_BRANCHES * D_MODEL
D_FF = ((8 * D_MODEL // 3 + 127) // 128) * 128
FFN_KSIZE = 3
RMS_EPS = 1e-6

kernel_name = 'hybrid_pool_shortconv_gated_block'


def rms_norm(x, g):
    xf = x.astype(jnp.float32)
    inv = lax.rsqrt(jnp.mean(xf * xf, axis=-1, keepdims=True) + RMS_EPS)
    return (xf * inv * g.astype(jnp.float32)).astype(x.dtype)


def causal_depthwise_conv(u, w):
    k, c = w.shape
    return lax.conv_general_dilated(
        u, w.astype(u.dtype)[:, None, :], window_strides=(1,), padding=[(k - 1, 0)],
        dimension_numbers=('NWC', 'WIO', 'NWC'), feature_group_count=c)


def causal_multiscale_pool(u):
    s = u.shape[1]
    pos = jnp.arange(1, s + 1, dtype=jnp.float32)
    outs = []
    for gi, win in enumerate(POOL_WINDOWS):
        ug = u[..., gi * POOL_GROUP_DIM:(gi + 1) * POOL_GROUP_DIM].astype(jnp.float32)
        csum = jnp.pad(jnp.cumsum(ug, axis=1), ((0, 0), (1, 0), (0, 0)))
        upper = csum[:, 1:]
        lower = jnp.pad(csum[:, :s + 1 - win], ((0, 0), (win - 1, 0), (0, 0)))
        count = jnp.minimum(pos, float(win))[None, :, None]
        outs.append(((upper - lower) / count - ug).astype(u.dtype))
    return jnp.stack(outs, axis=2)


def token_mixer(h, w_in, pool_w, pool_scale, w_pool_proj, conv_w, w_conv_out, w_o):
    b, s, _ = h.shape
    z = jnp.einsum('bsd,de->bse', h, w_in)
    splits = [POOL_WIDTH,
              POOL_WIDTH + CONV_WIDTH,
              POOL_WIDTH + 2 * CONV_WIDTH,
              POOL_WIDTH + 3 * CONV_WIDTH,
              POOL_WIDTH + 3 * CONV_WIDTH + D_MODEL]
    z_pool, z_b, z_c, z_v, z_gpool, z_gconv = jnp.split(z, splits, axis=-1)
    p = causal_multiscale_pool(z_pool)
    p = jnp.einsum('bsgc,gce->bsge', p, pool_w).reshape(b, s, POOL_WIDTH) * pool_scale
    y_pool = jnp.einsum('bsp,pd->bsd', p, w_pool_proj)
    y_conv = jnp.einsum('bsc,cd->bsd', z_b * causal_depthwise_conv(z_c * z_v, conv_w), w_conv_out)
    merged = jax.nn.sigmoid(z_gpool) * y_pool + jax.nn.sigmoid(z_gconv) * y_conv
    return jnp.einsum('bsd,de->bse', merged, w_o)


def channel_mixer(h, w_up, ffn_conv_w, ffn_conv_b, w_down):
    u = causal_depthwise_conv(jnp.einsum('bsd,df->bsf', h, w_up), ffn_conv_w) + ffn_conv_b
    gate, val = jnp.split(u, 2, axis=-1)
    return jnp.einsum('bsf,fd->bsd', jax.nn.silu(gate) * val, w_down)


def _normal(k, shape, scale):
    return jax.random.normal(k, shape, jnp.float32) * scale


def setup_inputs(seed: int = 0) -> dict:
    key = jax.random.key(seed)
    ks = jax.random.split(key, 15)
    return {
        'x': _normal(ks[0], (BATCH, SEQ, D_MODEL), 1.0),
        'norm_mix': 1.0 + _normal(ks[1], (DEPTH, D_MODEL), 0.1),
        'w_in': _normal(ks[2], (DEPTH, D_MODEL, D_IN), D_MODEL ** -0.5),
        'pool_w': _normal(ks[3], (DEPTH, N_POOL_GROUPS, POOL_GROUP_DIM, POOL_GROUP_DIM), POOL_GROUP_DIM ** -0.5),
        'pool_scale': 1.0 + _normal(ks[4], (DEPTH, POOL_WIDTH), 0.1),
        'w_pool_proj': _normal(ks[5], (DEPTH, POOL_WIDTH, D_MODEL), POOL_WIDTH ** -0.5),
        'conv_w': _normal(ks[6], (DEPTH, CONV_KSIZE, CONV_WIDTH), CONV_KSIZE ** -0.5),
        'w_conv_out': _normal(ks[7], (DEPTH, CONV_WIDTH, D_MODEL), CONV_WIDTH ** -0.5),
        'w_o': _normal(ks[8], (DEPTH, D_MODEL, D_MODEL), D_MODEL ** -0.5),
        'norm_ffn': 1.0 + _normal(ks[9], (DEPTH, D_MODEL), 0.1),
        'w_up': _normal(ks[10], (DEPTH, D_MODEL, 2 * D_FF), D_MODEL ** -0.5),
        'ffn_conv_w': _normal(ks[11], (DEPTH, FFN_KSIZE, 2 * D_FF), FFN_KSIZE ** -0.5),
        'ffn_conv_b': _normal(ks[12], (DEPTH, 2 * D_FF), 0.02),
        'w_down': _normal(ks[13], (DEPTH, D_FF, D_MODEL), D_FF ** -0.5),
        'norm_final': 1.0 + _normal(ks[14], (D_MODEL,), 0.1),
    }


def reference(x, norm_mix, w_in, pool_w, pool_scale, w_pool_proj, conv_w, w_conv_out, w_o,
              norm_ffn, w_up, ffn_conv_w, ffn_conv_b, w_down, norm_final):
    for layer in range(DEPTH):
        h = rms_norm(x, norm_mix[layer])
        x = x + token_mixer(h, w_in[layer], pool_w[layer], pool_scale[layer], w_pool_proj[layer],
                            conv_w[layer], w_conv_out[layer], w_o[layer])
        h = rms_norm(x, norm_ffn[layer])
        x = x + channel_mixer(h, w_up[layer], ffn_conv_w[layer], ffn_conv_b[layer], w_down[layer])
    return rms_norm(x, norm_final)
```

```python
import functools

import jax
import jax.numpy as jnp
from jax import lax
from jax.experimental import pallas as pl
from jax.experimental.pallas import tpu as pltpu

D_MODEL = 1024
N_POOL_GROUPS = 4
POOL_WINDOWS = (2, 4, 8, 16)
POOL_GROUP_DIM = D_MODEL // N_POOL_GROUPS
D_FF = 2816
KSIZE = 3
RMS_EPS = 1e-6

SEQ_TILE = 256
POOL_HIST = 16
CONV_HIST = 8
FFN_CHUNK = 256
V7X_VMEM_LIMIT_BYTES = 60 * 1024 * 1024


def _rms_norm(x, g):
    inv = lax.rsqrt(jnp.mean(x * x, axis=-1, keepdims=True) + RMS_EPS)
    return x * inv * g


def _dot(a, b):
    return jnp.dot(a, b, preferred_element_type=jnp.float32)


def _causal_taps(buf, hist, rows, cols, w_ref):
    out = None
    for k in range(KSIZE):
        start = hist - (KSIZE - 1) + k
        term = w_ref[k:k + 1, cols] * buf[start:start + rows, cols]
        out = term if out is None else out + term
    return out


def _block_kernel(x_ref, g_mix_ref, w_in_ref, pool_w_ref, pool_scale_ref, w_pool_proj_ref,
                  conv_w_ref, w_conv_out_ref, w_o_ref, g_ffn_ref, w_up_ref, ffn_conv_w_ref,
                  ffn_conv_b_ref, w_down_ref, g_final_ref, o_ref,
                  zp_buf, cv_buf, u_buf, act_buf):
    ts = x_ref.shape[0]
    t = pl.program_id(1)
    bf16 = jnp.bfloat16

    @pl.when(t == 0)
    def _():
        zp_buf[0:POOL_HIST, :] = jnp.zeros((POOL_HIST, D_MODEL), jnp.float32)
        cv_buf[0:CONV_HIST, :] = jnp.zeros((CONV_HIST, D_MODEL), jnp.float32)
        u_buf[0:CONV_HIST, :] = jnp.zeros((CONV_HIST, 2 * D_FF), jnp.float32)

    x = x_ref[...]
    h = _rms_norm(x, g_mix_ref[...]).astype(bf16)

    zp_buf[POOL_HIST:POOL_HIST + ts, :] = _dot(h, w_in_ref[:, 0:D_MODEL])
    pos = t * ts + lax.broadcasted_iota(jnp.int32, (ts, 1), 0) + 1
    p_groups = []
    for gi, win in enumerate(POOL_WINDOWS):
        cols = slice(gi * POOL_GROUP_DIM, (gi + 1) * POOL_GROUP_DIM)
        ug = zp_buf[POOL_HIST:POOL_HIST + ts, cols]
        s = ug
        for i in range(1, win):
            s = s + zp_buf[POOL_HIST - i:POOL_HIST - i + ts, cols]
        inv_count = 1.0 / jnp.minimum(pos, win).astype(jnp.float32)
        pooled = (s * inv_count - ug).astype(bf16)
        p_groups.append(_dot(pooled, pool_w_ref[gi]))
    zp_buf[0:POOL_HIST, :] = zp_buf[ts:ts + POOL_HIST, :]
    p = (jnp.concatenate(p_groups, axis=-1) * pool_scale_ref[...]).astype(bf16)
    y_pool = _dot(p, w_pool_proj_ref[...])

    z_c = _dot(h, w_in_ref[:, 2 * D_MODEL:3 * D_MODEL])
    z_v = _dot(h, w_in_ref[:, 3 * D_MODEL:4 * D_MODEL])
    cv_buf[CONV_HIST:CONV_HIST + ts, :] = z_c * z_v
    conv = _causal_taps(cv_buf, CONV_HIST, ts, slice(0, D_MODEL), conv_w_ref)
    cv_buf[0:CONV_HIST, :] = cv_buf[ts:ts + CONV_HIST, :]
    z_b = _dot(h, w_in_ref[:, D_MODEL:2 * D_MODEL])
    y_conv = _dot((z_b * conv).astype(bf16), w_conv_out_ref[...])

    z_gpool = _dot(h, w_in_ref[:, 4 * D_MODEL:5 * D_MODEL])
    z_gconv = _dot(h, w_in_ref[:, 5 * D_MODEL:6 * D_MODEL])
    merged = jax.nn.sigmoid(z_gpool) * y_pool + jax.nn.sigmoid(z_gconv) * y_conv
    x1 = x + _dot(merged.astype(bf16), w_o_ref[...])

    h2 = _rms_norm(x1, g_ffn_ref[...]).astype(bf16)
    u_buf[CONV_HIST:CONV_HIST + ts, :] = _dot(h2, w_up_ref[...])
    for c in range(D_FF // FFN_CHUNK):
        gcols = slice(c * FFN_CHUNK, (c + 1) * FFN_CHUNK)
        vcols = slice(D_FF + c * FFN_CHUNK, D_FF + (c + 1) * FFN_CHUNK)
        gate = _causal_taps(u_buf, CONV_HIST, ts, gcols, ffn_conv_w_ref) + ffn_conv_b_ref[:, gcols]
        val = _causal_taps(u_buf, CONV_HIST, ts, vcols, ffn_conv_w_ref) + ffn_conv_b_ref[:, vcols]
        act_buf[:, gcols] = (jax.nn.silu(gate) * val).astype(bf16)
    u_buf[0:CONV_HIST, :] = u_buf[ts:ts + CONV_HIST, :]
    x2 = x1 + _dot(act_buf[...], w_down_ref[...])

    o_ref[...] = _rms_norm(x2, g_final_ref[...])


def _resident(shape):
    return pl.BlockSpec(shape, lambda b, t: (0,) * len(shape), pipeline_mode=pl.Buffered(1))


@jax.jit
def kernel(x, norm_mix, w_in, pool_w, pool_scale, w_pool_proj, conv_w, w_conv_out, w_o,
           norm_ffn, w_up, ffn_conv_w, ffn_conv_b, w_down, norm_final):
    batch, seq, d_model = x.shape
    assert d_model == D_MODEL and seq % SEQ_TILE == 0 and D_FF % FFN_CHUNK == 0
    assert norm_mix.shape[0] == 1, "single-layer block"
    bf16 = jnp.bfloat16
    ts = SEQ_TILE

    operands = (
        x,
        norm_mix[0][None, :],
        w_in[0].astype(bf16),
        pool_w[0].astype(bf16),
        pool_scale[0][None, :],
        w_pool_proj[0].astype(bf16),
        conv_w[0],
        w_conv_out[0].astype(bf16),
        w_o[0].astype(bf16),
        norm_ffn[0][None, :],
        w_up[0].astype(bf16),
        ffn_conv_w[0],
        ffn_conv_b[0][None, :],
        w_down[0].astype(bf16),
        norm_final[None, :],
    )
    x_spec = pl.BlockSpec((None, ts, D_MODEL), lambda b, t: (b, t, 0))
    in_specs = [x_spec] + [_resident(op.shape) for op in operands[1:]]

    return pl.pallas_call(
        _block_kernel,
        grid=(batch, seq // ts),
        in_specs=in_specs,
        out_specs=x_spec,
        out_shape=jax.ShapeDtypeStruct(x.shape, x.dtype),
        scratch_shapes=[
            pltpu.VMEM((POOL_HIST + ts, D_MODEL), jnp.float32),
            pltpu.VMEM((CONV_HIST + ts, D_MODEL), jnp.float32),
            pltpu.VMEM((CONV_HIST + ts, 2 * D_FF), jnp.float32),
            pltpu.VMEM((ts, D_FF), bf16),
        ],
        compiler_params=pltpu.CompilerParams(
            dimension_semantics=("arbitrary", "arbitrary"),
            vmem_limit_bytes=V7X_VMEM_LIMIT_BYTES,
        ),
        name="hybrid_block",
    )(*operands)
```

```python
import functools

import jax
import jax.numpy as jnp
from jax import lax
from jax.experimental import pallas as pl
from jax.experimental.pallas import tpu as pltpu

D_MODEL = 1024
N_POOL_GROUPS = 4
POOL_WINDOWS = (2, 4, 8, 16)
POOL_GROUP_DIM = D_MODEL // N_POOL_GROUPS
MXU_TILE = 256
D_FF = 2816
KSIZE = 3
RMS_EPS = 1e-6

SEQ_TILE = 256
POOL_HIST = 16
CONV_HIST = 8
FFN_CHUNK = 256
N_FFN_CHUNKS = D_FF // FFN_CHUNK
UP_AHEAD = 2
V7X_VMEM_LIMIT_BYTES = 60 * 1024 * 1024


def _rms_norm(x, g):
    inv = lax.rsqrt(jnp.mean(x * x, axis=-1, keepdims=True) + RMS_EPS)
    return x * inv * g


def _dot(a, b):
    return jnp.dot(a, b, preferred_element_type=jnp.float32)


LANES = 128


def _slab_store(buf, row0, val):
    for j in range(buf.shape[0]):
        buf[j, row0:row0 + val.shape[0], :] = val[:, j * LANES:(j + 1) * LANES]


def _slab_rows(buf, j, start, rows):
    return buf[pl.ds(j, 1, stride=2), pl.ds(start, rows), :][0]


def _slab_window(buf, start, rows):
    return jnp.concatenate([_slab_rows(buf, j, start, rows) for j in range(buf.shape[0])], axis=-1)


def _causal_taps(buf, hist, rows, w_ref, w_col0):
    w_cols = slice(w_col0, w_col0 + buf.shape[0] * LANES)
    out = None
    for k in range(KSIZE):
        term = w_ref[k:k + 1, w_cols] * _slab_window(buf, hist - (KSIZE - 1) + k, rows)
        out = term if out is None else out + term
    return out


def _block_kernel(x_ref, g_mix_ref, w_in_ref, pool_w_ref, pool_scale_ref, w_pool_proj_ref,
                  conv_w_ref, w_conv_out_ref, w_o_ref, g_ffn_ref, w_up_ref, ffn_conv_w_ref,
                  ffn_conv_b_ref, w_down_ref, g_final_ref, o_ref,
                  zp_buf, cv_buf, *u_bufs):
    ts = x_ref.shape[0]
    t = pl.program_id(1)
    bf16 = jnp.bfloat16

    @pl.when(t == 0)
    def _():
        zp_buf[:, 0:POOL_HIST, :] = jnp.zeros((zp_buf.shape[0], POOL_HIST, LANES), jnp.float32)
        for buf in (cv_buf,) + u_bufs:
            buf[:, 0:CONV_HIST, :] = jnp.zeros((buf.shape[0], CONV_HIST, LANES), jnp.float32)

    x = x_ref[...]
    h = _rms_norm(x, g_mix_ref[...]).astype(bf16)

    def in_proj(seg):
        return _dot(h, w_in_ref[:, seg * D_MODEL:(seg + 1) * D_MODEL])

    def blocked_dot(lhs_block, w_ref):
        out = None
        for j in range(w_ref.shape[0] // MXU_TILE):
            part = _dot(lhs_block(j), w_ref[j * MXU_TILE:(j + 1) * MXU_TILE, :])
            out = part if out is None else out + part
        return out

    def block(j):
        return slice(j * MXU_TILE, (j + 1) * MXU_TILE)

    _slab_store(zp_buf, POOL_HIST, in_proj(0))
    z_c = in_proj(2)
    z_v = in_proj(3)

    pos = t * ts + lax.broadcasted_iota(jnp.int32, (ts, 1), 0) + 1
    p_groups = []
    slabs_per_group = POOL_GROUP_DIM // LANES
    for gi, win in enumerate(POOL_WINDOWS):
        def window(back):
            return jnp.concatenate(
                [_slab_rows(zp_buf, gi * slabs_per_group + j, POOL_HIST - back, ts)
                 for j in range(slabs_per_group)], axis=-1)
        ug = window(0)
        s = ug
        for i in range(1, win):
            s = s + window(i)
        inv_count = 1.0 / jnp.minimum(pos, win).astype(jnp.float32)
        pooled = (s * inv_count - ug).astype(bf16)
        p_groups.append((_dot(pooled, pool_w_ref[gi]) * pool_scale_ref[:, block(gi)]).astype(bf16))
    zp_buf[:, 0:POOL_HIST, :] = zp_buf[:, ts:ts + POOL_HIST, :]

    _slab_store(cv_buf, CONV_HIST, z_c * z_v)
    conv = _causal_taps(cv_buf, CONV_HIST, ts, conv_w_ref, 0)
    cv_buf[:, 0:CONV_HIST, :] = cv_buf[:, ts:ts + CONV_HIST, :]
    conv_in = (in_proj(1) * conv).astype(bf16)

    y_pool = blocked_dot(lambda j: p_groups[j], w_pool_proj_ref)
    z_gpool = in_proj(4)
    y_conv = blocked_dot(lambda j: conv_in[:, block(j)], w_conv_out_ref)
    z_gconv = in_proj(5)

    merged = (jax.nn.sigmoid(z_gpool) * y_pool + jax.nn.sigmoid(z_gconv) * y_conv).astype(bf16)
    x1 = x + blocked_dot(lambda j: merged[:, block(j)], w_o_ref)

    h2 = _rms_norm(x1, g_ffn_ref[...]).astype(bf16)
    def up_proj(c):
        cols = slice(2 * c * FFN_CHUNK, 2 * (c + 1) * FFN_CHUNK)
        _slab_store(u_bufs[c], CONV_HIST, _dot(h2, w_up_ref[:, cols]))

    for c in range(UP_AHEAD):
        up_proj(c)
    x2 = x1
    for c, u_buf in enumerate(u_bufs):
        cols = slice(2 * c * FFN_CHUNK, 2 * (c + 1) * FFN_CHUNK)
        u = _causal_taps(u_buf, CONV_HIST, ts, ffn_conv_w_ref, cols.start) + ffn_conv_b_ref[:, cols]
        u_buf[:, 0:CONV_HIST, :] = u_buf[:, ts:ts + CONV_HIST, :]
        act = (jax.nn.silu(u[:, :FFN_CHUNK]) * u[:, FFN_CHUNK:]).astype(bf16)
        if c + UP_AHEAD < N_FFN_CHUNKS:
            up_proj(c + UP_AHEAD)
        x2 = x2 + _dot(act, w_down_ref[c * FFN_CHUNK:(c + 1) * FFN_CHUNK, :])

    o_ref[...] = _rms_norm(x2, g_final_ref[...])


def _resident(shape):
    return pl.BlockSpec(shape, lambda b, t: (0,) * len(shape), pipeline_mode=pl.Buffered(1))


def _chunk_interleave(a):
    lead = a.shape[:-1]
    a = a.reshape(lead + (2, N_FFN_CHUNKS, FFN_CHUNK))
    return jnp.swapaxes(a, -3, -2).reshape(lead + (2 * D_FF,))


@jax.jit
def kernel(x, norm_mix, w_in, pool_w, pool_scale, w_pool_proj, conv_w, w_conv_out, w_o,
           norm_ffn, w_up, ffn_conv_w, ffn_conv_b, w_down, norm_final):
    batch, seq, d_model = x.shape
    assert d_model == D_MODEL and seq % SEQ_TILE == 0 and D_FF % FFN_CHUNK == 0
    assert norm_mix.shape[0] == 1, "single-layer block"
    bf16 = jnp.bfloat16
    ts = SEQ_TILE

    operands = (
        x,
        norm_mix[0][None, :],
        w_in[0].astype(bf16),
        pool_w[0].astype(bf16),
        pool_scale[0][None, :],
        w_pool_proj[0].astype(bf16),
        conv_w[0],
        w_conv_out[0].astype(bf16),
        w_o[0].astype(bf16),
        norm_ffn[0][None, :],
        _chunk_interleave(w_up[0]).astype(bf16),
        _chunk_interleave(ffn_conv_w[0]),
        _chunk_interleave(ffn_conv_b[0])[None, :],
        w_down[0].astype(bf16),
        norm_final[None, :],
    )
    x_spec = pl.BlockSpec((None, ts, D_MODEL), lambda b, t: (b, t, 0))
    in_specs = [x_spec] + [_resident(op.shape) for op in operands[1:]]

    return pl.pallas_call(
        _block_kernel,
        grid=(batch, seq // ts),
        in_specs=in_specs,
        out_specs=x_spec,
        out_shape=jax.ShapeDtypeStruct(x.shape, x.dtype),
        scratch_shapes=[
            pltpu.VMEM((D_MODEL // LANES, POOL_HIST + ts, LANES), jnp.float32),
            pltpu.VMEM((D_MODEL // LANES, CONV_HIST + ts, LANES), jnp.float32),
        ] + [
            pltpu.VMEM((2 * FFN_CHUNK // LANES, CONV_HIST + ts, LANES), jnp.float32)
            for _ in range(N_FFN_CHUNKS)
        ],
        compiler_params=pltpu.CompilerParams(
            dimension_semantics=("arbitrary", "arbitrary"),
            vmem_limit_bytes=V7X_VMEM_LIMIT_BYTES,
        ),
        name="hybrid_block",
    )(*operands)
```

```python
import functools

import jax
import jax.numpy as jnp
from jax import lax
from jax.experimental import pallas as pl
from jax.experimental.pallas import tpu as pltpu

D_MODEL = 1024
N_POOL_GROUPS = 4
POOL_WINDOWS = (2, 4, 8, 16)
POOL_GROUP_DIM = D_MODEL // N_POOL_GROUPS
MXU_TILE = 256
D_FF = 2816
KSIZE = 3
RMS_EPS = 1e-6
LANES = 128

SEQ_TILE = 256
POOL_HIST = 16
CONV_HIST = 8
FFN_CHUNK = 256
N_FFN_CHUNKS = D_FF // FFN_CHUNK
UP_AHEAD = 2
V7X_VMEM_LIMIT_BYTES = 60 * 1024 * 1024


def _rms_norm(x, g):
    inv = lax.rsqrt(jnp.mean(x * x, axis=-1, keepdims=True) + RMS_EPS)
    return x * inv * g


def _dot(a, b):
    return jnp.dot(a, b, preferred_element_type=jnp.float32)


def _slab_store(buf, row0, val):
    for j in range(buf.shape[0]):
        buf[j, row0:row0 + val.shape[0], :] = val[:, j * LANES:(j + 1) * LANES]


def _slab_rows(buf, j, start, rows):
    return buf[pl.ds(j, 1, stride=2), pl.ds(start, rows), :][0]


def _slab_window(buf, start, rows):
    return jnp.concatenate([_slab_rows(buf, j, start, rows) for j in range(buf.shape[0])], axis=-1)


def _causal_taps(buf, hist, rows, w):
    out = None
    for k in range(KSIZE):
        term = w[k:k + 1, :] * _slab_window(buf, hist - (KSIZE - 1) + k, rows)
        out = term if out is None else out + term
    return out


def _block(j):
    return slice(j * MXU_TILE, (j + 1) * MXU_TILE)


def _block_kernel(tiles_per_row, x_ref, g_mix_ref, w_in_ref, pool_w_ref, pool_scale_ref,
                  w_pool_proj_ref, conv_w_ref, w_conv_out_ref, w_o_ref, g_ffn_ref, w_up_ref,
                  ffn_conv_w_ref, ffn_conv_b_ref, w_down_ref, g_final_ref, o_ref,
                  zp_buf, cv_buf, x1_buf, h2_buf, *u_bufs):
    ts = x_ref.shape[0]
    bf16 = jnp.bfloat16
    i = pl.program_id(0)
    n_tiles = pl.num_programs(0) - 1
    t_tok = lax.rem(jnp.minimum(i, n_tiles - 1), tiles_per_row)
    t_ch = lax.rem(jnp.maximum(i - 1, 0), tiles_per_row)

    @pl.when(t_tok == 0)
    def _():
        zp_buf[:, 0:POOL_HIST, :] = jnp.zeros((zp_buf.shape[0], POOL_HIST, LANES), jnp.float32)
        cv_buf[:, 0:CONV_HIST, :] = jnp.zeros((cv_buf.shape[0], CONV_HIST, LANES), jnp.float32)

    @pl.when(t_ch == 0)
    def _():
        for buf in u_bufs:
            buf[:, 0:CONV_HIST, :] = jnp.zeros((buf.shape[0], CONV_HIST, LANES), jnp.float32)

    @pl.when(i == 0)
    def _():
        x1_buf[...] = jnp.zeros(x1_buf.shape, jnp.float32)
        h2_buf[...] = jnp.zeros(h2_buf.shape, bf16)

    def gate_val(ref, c):
        gate = ref[:, c * FFN_CHUNK:(c + 1) * FFN_CHUNK]
        val = ref[:, D_FF + c * FFN_CHUNK:D_FF + (c + 1) * FFN_CHUNK]
        return jnp.concatenate([gate, val], axis=1)

    def up_proj(c):
        _slab_store(u_bufs[c], CONV_HIST, _dot(h2_buf[...], gate_val(w_up_ref, c)))

    def ffn_act(c):
        u_buf = u_bufs[c]
        u = (_causal_taps(u_buf, CONV_HIST, ts, gate_val(ffn_conv_w_ref, c))
             + gate_val(ffn_conv_b_ref, c))
        u_buf[:, 0:CONV_HIST, :] = u_buf[:, ts:ts + CONV_HIST, :]
        return (jax.nn.silu(u[:, :FFN_CHUNK]) * u[:, FFN_CHUNK:]).astype(bf16)

    def down_proj(c, act):
        return _dot(act, w_down_ref[c * FFN_CHUNK:(c + 1) * FFN_CHUNK, :])

    def in_proj(seg):
        return _dot(h, w_in_ref[:, seg * D_MODEL:(seg + 1) * D_MODEL])

    def blocked_dot(lhs_block, w_ref, blocks):
        out = None
        for j in blocks:
            part = _dot(lhs_block(j), w_ref[_block(j), :])
            out = part if out is None else out + part
        return out

    all_blocks = range(D_MODEL // MXU_TILE)

    def pool_windows():
        pos = t_tok * ts + lax.broadcasted_iota(jnp.int32, (ts, 1), 0) + 1
        slabs_per_group = POOL_GROUP_DIM // LANES
        pooled = []
        for gi, win in enumerate(POOL_WINDOWS):
            def window(back):
                return jnp.concatenate(
                    [_slab_rows(zp_buf, gi * slabs_per_group + j, POOL_HIST - back, ts)
                     for j in range(slabs_per_group)], axis=-1)
            ug = window(0)
            s = ug
            for k in range(1, win):
                s = s + window(k)
            inv_count = 1.0 / jnp.minimum(pos, win).astype(jnp.float32)
            pooled.append((s * inv_count - ug).astype(bf16))
        zp_buf[:, 0:POOL_HIST, :] = zp_buf[:, ts:ts + POOL_HIST, :]
        return pooled

    up_proj(0)
    up_proj(1)
    x = x_ref[...]
    h = _rms_norm(x, g_mix_ref[...]).astype(bf16)
    _slab_store(zp_buf, POOL_HIST, in_proj(0))

    v = {}

    def t_conv_c():
        v["z_c"] = in_proj(2)

    def t_conv_v_and_windows():
        z_v = in_proj(3)
        v["pooled"] = pool_windows()
        _slab_store(cv_buf, CONV_HIST, v["z_c"] * z_v)
        v["conv"] = _causal_taps(cv_buf, CONV_HIST, ts, conv_w_ref[...])
        cv_buf[:, 0:CONV_HIST, :] = cv_buf[:, ts:ts + CONV_HIST, :]

    def t_pool_groups():
        v["p_groups"] = [
            (_dot(v["pooled"][gi], pool_w_ref[gi]) * pool_scale_ref[:, _block(gi)]).astype(bf16)
            for gi in range(N_POOL_GROUPS)]

    def t_conv_b():
        v["conv_in"] = (in_proj(1) * v["conv"]).astype(bf16)

    def t_pool_proj():
        v["y_pool"] = blocked_dot(lambda j: v["p_groups"][j], w_pool_proj_ref, all_blocks)

    def t_gate_pool():
        v["gated_pool"] = jax.nn.sigmoid(in_proj(4)) * v["y_pool"]

    def t_gate_conv():
        v["gate_conv"] = jax.nn.sigmoid(in_proj(5))

    def t_conv_out():
        y_conv = blocked_dot(lambda j: v["conv_in"][:, _block(j)], w_conv_out_ref, all_blocks)
        v["merged"] = (v["gated_pool"] + v["gate_conv"] * y_conv).astype(bf16)

    def t_out_proj_a():
        v["x1"] = x + blocked_dot(lambda j: v["merged"][:, _block(j)], w_o_ref, (0, 1))

    def t_out_proj_b():
        x1 = v["x1"] + blocked_dot(lambda j: v["merged"][:, _block(j)], w_o_ref, (2, 3))
        x1_buf[...] = x1
        h2_buf[...] = _rms_norm(x1, g_ffn_ref[...]).astype(bf16)

    token_pieces = (t_conv_c, None, t_conv_v_and_windows, None, t_pool_groups, t_conv_b,
                    t_pool_proj, t_gate_pool, t_gate_conv, t_conv_out, t_out_proj_a)
    x2 = x1_buf[...]
    for c in range(N_FFN_CHUNKS):
        act = ffn_act(c)
        if c + UP_AHEAD < N_FFN_CHUNKS:
            up_proj(c + UP_AHEAD)
        if token_pieces[c] is not None:
            token_pieces[c]()
        x2 = x2 + down_proj(c, act)
    o_ref[...] = _rms_norm(x2, g_final_ref[...])
    t_out_proj_b()


def _resident(shape):
    return pl.BlockSpec(shape, lambda i: (0,) * len(shape), pipeline_mode=pl.Buffered(1))


@jax.jit
def kernel(x, norm_mix, w_in, pool_w, pool_scale, w_pool_proj, conv_w, w_conv_out, w_o,
           norm_ffn, w_up, ffn_conv_w, ffn_conv_b, w_down, norm_final):
    batch, seq, d_model = x.shape
    assert d_model == D_MODEL and seq % SEQ_TILE == 0 and D_FF % FFN_CHUNK == 0
    assert POOL_GROUP_DIM == MXU_TILE and N_FFN_CHUNKS == 11
    assert norm_mix.shape[0] == 1, "single-layer block"
    bf16 = jnp.bfloat16
    ts = SEQ_TILE
    tiles_per_row = seq // ts
    n_tiles = batch * tiles_per_row

    operands = (
        x,
        norm_mix[0][None, :],
        w_in[0].astype(bf16),
        pool_w[0].astype(bf16),
        pool_scale[0][None, :],
        w_pool_proj[0].astype(bf16),
        conv_w[0],
        w_conv_out[0].astype(bf16),
        w_o[0].astype(bf16),
        norm_ffn[0][None, :],
        w_up[0].astype(bf16),
        ffn_conv_w[0],
        ffn_conv_b[0][None, :],
        w_down[0].astype(bf16),
        norm_final[None, :],
    )

    def tile_index(tile):
        return tile // tiles_per_row, tile % tiles_per_row, 0

    x_spec = pl.BlockSpec((None, ts, D_MODEL), lambda i: tile_index(jnp.minimum(i, n_tiles - 1)))
    o_spec = pl.BlockSpec((None, ts, D_MODEL), lambda i: tile_index(jnp.maximum(i - 1, 0)))
    in_specs = [x_spec] + [_resident(op.shape) for op in operands[1:]]

    return pl.pallas_call(
        functools.partial(_block_kernel, tiles_per_row),
        grid=(n_tiles + 1,),
        in_specs=in_specs,
        out_specs=o_spec,
        out_shape=jax.ShapeDtypeStruct(x.shape, x.dtype),
        scratch_shapes=[
            pltpu.VMEM((D_MODEL // LANES, POOL_HIST + ts, LANES), jnp.float32),
            pltpu.VMEM((D_MODEL // LANES, CONV_HIST + ts, LANES), jnp.float32),
            pltpu.VMEM((ts, D_MODEL), jnp.float32),
            pltpu.VMEM((ts, D_MODEL), bf16),
        ] + [
            pltpu.VMEM((2 * FFN_CHUNK // LANES, CONV_HIST + ts, LANES), jnp.float32)
            for _ in range(N_FFN_CHUNKS)
        ],
        compiler_params=pltpu.CompilerParams(
            dimension_semantics=("arbitrary",),
            vmem_limit_bytes=V7X_VMEM_LIMIT_BYTES,
        ),
        name="hybrid_block",
    )(*operands)
```

```python
import functools

import jax
import jax.numpy as jnp
from jax import lax
from jax.experimental import pallas as pl
from jax.experimental.pallas import tpu as pltpu

D_MODEL = 1024
N_POOL_GROUPS = 4
POOL_WINDOWS = (2, 4, 8, 16)
POOL_GROUP_DIM = D_MODEL // N_POOL_GROUPS
MXU_TILE = 256
D_FF = 2816
KSIZE = 3
RMS_EPS = 1e-6
LANES = 128
BF16_ROWS = 16

SEQ_TILE = 256
TILES_PER_STEP = 1
POOL_HIST = 16
CONV_HIST = 8
FFN_CHUNK = 256
N_FFN_CHUNKS = D_FF // FFN_CHUNK
UP_AHEAD = 2
CAST_CHUNK_BYTES = 1024 * 1024
V7X_VMEM_LIMIT_BYTES = 60 * 1024 * 1024


def _rms_norm(x, g):
    inv = lax.rsqrt(jnp.mean(x * x, axis=-1, keepdims=True) + RMS_EPS)
    return x * inv * g


def _dot(a, b):
    return jnp.dot(a, b, preferred_element_type=jnp.float32)


def _slab_store(buf, row0, val):
    for j in range(buf.shape[0]):
        buf[j, row0:row0 + val.shape[0], :] = val[:, j * LANES:(j + 1) * LANES]


def _slab_rows(buf, j, start, rows):
    return buf[pl.ds(j, 1, stride=2), pl.ds(start, rows), :][0]


def _slab_window(buf, start, rows):
    return jnp.concatenate([_slab_rows(buf, j, start, rows) for j in range(buf.shape[0])], axis=-1)


def _carry_history(buf, hist, rows, keep):
    buf[:, 0:hist, :] = jnp.where(keep, buf[:, rows:rows + hist, :], 0.0)


def _causal_taps(buf, hist, rows, w):
    out = None
    for k in range(KSIZE):
        term = w[k:k + 1, :] * _slab_window(buf, hist - (KSIZE - 1) + k, rows)
        out = term if out is None else out + term
    return out


def _block(j):
    return slice(j * MXU_TILE, (j + 1) * MXU_TILE)


def _tile_pair(x_ref, o_ref, x1_buf, h2_buf, t_tok, t_ch, tiles_per_row,
               g_mix_ref, w_in_ref, pool_w_ref, pool_scale_ref, w_pool_proj_ref, conv_w_ref,
               w_conv_out_ref, w_o_ref, g_ffn_ref, w_up_ref, ffn_conv_w_ref, ffn_conv_b_ref,
               w_down_ref, g_final_ref, zp_buf, cv_buf, u_bufs):
    ts = x_ref.shape[0]
    bf16 = jnp.bfloat16
    tok_continues = t_tok + 1 < tiles_per_row
    ch_continues = t_ch + 1 < tiles_per_row

    def gate_val(ref, c):
        gate = ref[:, c * FFN_CHUNK:(c + 1) * FFN_CHUNK]
        val = ref[:, D_FF + c * FFN_CHUNK:D_FF + (c + 1) * FFN_CHUNK]
        return jnp.concatenate([gate, val], axis=1)

    def up_proj(c):
        _slab_store(u_bufs[c], CONV_HIST, _dot(h2_buf[...], gate_val(w_up_ref, c)))

    def ffn_act(c):
        u_buf = u_bufs[c]
        u = (_causal_taps(u_buf, CONV_HIST, ts, gate_val(ffn_conv_w_ref, c))
             + gate_val(ffn_conv_b_ref, c))
        _carry_history(u_buf, CONV_HIST, ts, ch_continues)
        return (jax.nn.silu(u[:, :FFN_CHUNK]) * u[:, FFN_CHUNK:]).astype(bf16)

    def down_proj(c, act):
        return _dot(act, w_down_ref[c * FFN_CHUNK:(c + 1) * FFN_CHUNK, :])

    def in_proj(seg):
        return _dot(v["h"], w_in_ref[:, seg * D_MODEL:(seg + 1) * D_MODEL])

    def blocked_dot(lhs_block, w_ref, blocks):
        out = None
        for j in blocks:
            part = _dot(lhs_block(j), w_ref[_block(j), :])
            out = part if out is None else out + part
        return out

    all_blocks = range(D_MODEL // MXU_TILE)

    def pool_windows():
        pos = t_tok * ts + lax.broadcasted_iota(jnp.int32, (ts, 1), 0) + 1
        slabs_per_group = POOL_GROUP_DIM // LANES
        pooled = []
        for gi, win in enumerate(POOL_WINDOWS):
            def window(back):
                return jnp.concatenate(
                    [_slab_rows(zp_buf, gi * slabs_per_group + j, POOL_HIST - back, ts)
                     for j in range(slabs_per_group)], axis=-1)
            ug = window(0)
            s = ug
            for k in range(1, win):
                s = s + window(k)
            inv_count = 1.0 / jnp.minimum(pos, win).astype(jnp.float32)
            pooled.append((s * inv_count - ug).astype(bf16))
        _carry_history(zp_buf, POOL_HIST, ts, tok_continues)
        return pooled

    v = {}

    def t_conv_c():
        v["z_c"] = in_proj(2)

    def t_conv_v_and_windows():
        z_v = in_proj(3)
        v["pooled"] = pool_windows()
        _slab_store(cv_buf, CONV_HIST, v["z_c"] * z_v)
        v["conv"] = _causal_taps(cv_buf, CONV_HIST, ts, conv_w_ref[...])
        _carry_history(cv_buf, CONV_HIST, ts, tok_continues)

    def t_pool_groups():
        v["p_groups"] = [
            (_dot(v["pooled"][gi], pool_w_ref[_block(gi), :])
             * pool_scale_ref[:, _block(gi)]).astype(bf16)
            for gi in range(N_POOL_GROUPS)]

    def t_conv_b():
        v["conv_in"] = (in_proj(1) * v["conv"]).astype(bf16)

    def t_pool_proj():
        v["y_pool"] = blocked_dot(lambda j: v["p_groups"][j], w_pool_proj_ref, all_blocks)

    def t_gate_pool():
        v["gated_pool"] = jax.nn.sigmoid(in_proj(4)) * v["y_pool"]

    def t_gate_conv():
        v["gate_conv"] = jax.nn.sigmoid(in_proj(5))

    def t_conv_out():
        y_conv = blocked_dot(lambda j: v["conv_in"][:, _block(j)], w_conv_out_ref, all_blocks)
        v["merged"] = (v["gated_pool"] + v["gate_conv"] * y_conv).astype(bf16)

    def t_out_proj_a():
        v["x1"] = v["x"] + blocked_dot(lambda j: v["merged"][:, _block(j)], w_o_ref, (0, 1))

    def t_out_proj_b():
        x1 = v["x1"] + blocked_dot(lambda j: v["merged"][:, _block(j)], w_o_ref, (2, 3))
        x1_buf[...] = x1
        h2_buf[...] = _rms_norm(x1, g_ffn_ref[...]).astype(bf16)

    def up_head():
        for c in range(UP_AHEAD):
            up_proj(c)

    def token_head():
        v["x"] = x_ref[...]
        v["h"] = _rms_norm(v["x"], g_mix_ref[...]).astype(bf16)
        _slab_store(zp_buf, POOL_HIST, in_proj(0))

    def main():
        token_pieces = (t_conv_c, None, t_conv_v_and_windows, None, t_pool_groups, t_conv_b,
                        t_pool_proj, t_gate_pool, t_gate_conv, t_conv_out, t_out_proj_a)
        x2 = x1_buf[...]
        for c in range(N_FFN_CHUNKS):
            act = ffn_act(c)
            if c + UP_AHEAD < N_FFN_CHUNKS:
                up_proj(c + UP_AHEAD)
            if token_pieces[c] is not None:
                token_pieces[c]()
            x2 = x2 + down_proj(c, act)
        v["x2"] = x2

    def tail():
        o_ref[...] = _rms_norm(v["x2"], g_final_ref[...])
        t_out_proj_b()

    return up_head, token_head, main, tail


def _cast_chunk_rows(rows, cols):
    budget = max(BF16_ROWS, CAST_CHUNK_BYTES // (4 * cols))
    return max(r for r in range(BF16_ROWS, min(rows, budget) + 1, BF16_ROWS) if rows % r == 0)


def _load_as_bf16(src_hbm, dst, chunk_rows):
    rows, cols = src_hbm.shape
    assert rows % chunk_rows == 0 and dst.shape == src_hbm.shape
    n_chunks = rows // chunk_rows

    def scoped(stage, sems):
        def chunk_copy(i, slot):
            return pltpu.make_async_copy(
                src_hbm.at[pl.ds(i * chunk_rows, chunk_rows), :], stage.at[slot], sems.at[slot])

        chunk_copy(0, 0).start()

        def body(i, carry):
            slot = lax.rem(i, 2)

            @pl.when(i + 1 < n_chunks)
            def _():
                chunk_copy(i + 1, 1 - slot).start()

            chunk_copy(i, slot).wait()
            row0 = pl.multiple_of(i * chunk_rows, chunk_rows)
            dst[pl.ds(row0, chunk_rows), :] = stage[slot].astype(jnp.bfloat16)
            return carry

        lax.fori_loop(0, n_chunks, body, 0)

    pl.run_scoped(scoped, pltpu.VMEM((2, chunk_rows, cols), jnp.float32),
                  pltpu.SemaphoreType.DMA((2,)))


def _block_kernel(tiles_per_row, x_ref, *refs):
    (g_mix_ref, w_in_hbm, pool_w_hbm, pool_scale_ref, w_pool_proj_hbm, conv_w_ref,
     w_conv_out_hbm, w_o_hbm, g_ffn_ref, w_up_hbm, ffn_conv_w_ref, ffn_conv_b_ref, w_down_hbm,
     g_final_ref, o_ref, zp_buf, cv_buf, x1_buf, h2_buf,
     w_in_ref, pool_w_ref, w_pool_proj_ref, w_conv_out_ref, w_o_ref, w_up_ref,
     w_down_ref) = refs[:26]
    u_bufs = refs[26:]
    weights = (g_mix_ref, w_in_ref, pool_w_ref, pool_scale_ref, w_pool_proj_ref, conv_w_ref,
               w_conv_out_ref, w_o_ref, g_ffn_ref, w_up_ref, ffn_conv_w_ref, ffn_conv_b_ref,
               w_down_ref, g_final_ref)
    ts = x_ref.shape[0] // TILES_PER_STEP
    step = pl.program_id(0)
    n_steps = pl.num_programs(0) - 1
    first_tok = jnp.minimum(step, n_steps - 1) * TILES_PER_STEP
    first_ch = jnp.maximum(step - 1, 0) * TILES_PER_STEP

    @pl.when(step == 0)
    def _():
        zp_buf[:, 0:POOL_HIST, :] = jnp.zeros((zp_buf.shape[0], POOL_HIST, LANES), jnp.float32)
        for buf in (cv_buf,) + tuple(u_bufs):
            buf[:, 0:CONV_HIST, :] = jnp.zeros((buf.shape[0], CONV_HIST, LANES), jnp.float32)
        x1_buf[...] = jnp.zeros(x1_buf.shape, jnp.float32)
        h2_buf[...] = jnp.zeros(h2_buf.shape, jnp.bfloat16)
        for src, dst in ((w_up_hbm, w_up_ref), (w_in_hbm, w_in_ref), (w_down_hbm, w_down_ref),
                         (pool_w_hbm, pool_w_ref), (w_pool_proj_hbm, w_pool_proj_ref),
                         (w_conv_out_hbm, w_conv_out_ref), (w_o_hbm, w_o_ref)):
            _load_as_bf16(src, dst, _cast_chunk_rows(*src.shape))

    pairs = []
    for s in range(TILES_PER_STEP):
        rows = pl.ds(s * ts, ts)
        pairs.append(_tile_pair(
            x_ref.at[rows], o_ref.at[rows], x1_buf.at[s], h2_buf.at[s],
            lax.rem(first_tok + s, tiles_per_row), lax.rem(first_ch + s, tiles_per_row),
            tiles_per_row, *weights, zp_buf, cv_buf, u_bufs))
    up_head, token_head, _, _ = pairs[0]
    up_head()
    token_head()
    for s, (_, _, main, tail) in enumerate(pairs):
        main()
        if s + 1 < TILES_PER_STEP:
            pairs[s + 1][0]()
        tail()
        if s + 1 < TILES_PER_STEP:
            pairs[s + 1][1]()


def _resident(shape):
    return pl.BlockSpec(shape, lambda i: (0,) * len(shape), pipeline_mode=pl.Buffered(1))


@jax.jit
def kernel(x, norm_mix, w_in, pool_w, pool_scale, w_pool_proj, conv_w, w_conv_out, w_o,
           norm_ffn, w_up, ffn_conv_w, ffn_conv_b, w_down, norm_final):
    batch, seq, d_model = x.shape
    ts = SEQ_TILE
    step_rows = ts * TILES_PER_STEP
    assert d_model == D_MODEL and seq % step_rows == 0 and D_FF % FFN_CHUNK == 0
    assert POOL_GROUP_DIM == MXU_TILE and N_FFN_CHUNKS == 11
    assert norm_mix.shape[0] == 1, "single-layer block"
    bf16 = jnp.bfloat16
    tiles_per_row = seq // ts
    steps_per_row = seq // step_rows
    n_steps = batch * steps_per_row

    matmul_weights = {2, 3, 5, 7, 8, 10, 13}
    operands = (
        x,
        norm_mix[0][None, :],
        w_in[0],
        pool_w[0].reshape(N_POOL_GROUPS * POOL_GROUP_DIM, POOL_GROUP_DIM),
        pool_scale[0][None, :],
        w_pool_proj[0],
        conv_w[0],
        w_conv_out[0],
        w_o[0],
        norm_ffn[0][None, :],
        w_up[0],
        ffn_conv_w[0],
        ffn_conv_b[0][None, :],
        w_down[0],
        norm_final[None, :],
    )

    def block_index(blk):
        return blk // steps_per_row, blk % steps_per_row, 0

    x_spec = pl.BlockSpec((None, step_rows, D_MODEL), lambda i: block_index(jnp.minimum(i, n_steps - 1)))
    o_spec = pl.BlockSpec((None, step_rows, D_MODEL), lambda i: block_index(jnp.maximum(i - 1, 0)))
    in_specs = [x_spec] + [
        pl.BlockSpec(memory_space=pl.ANY) if k in matmul_weights else _resident(op.shape)
        for k, op in enumerate(operands) if k > 0]
    resident_bf16 = [pltpu.VMEM(operands[k].shape, bf16) for k in sorted(matmul_weights)]

    return pl.pallas_call(
        functools.partial(_block_kernel, tiles_per_row),
        grid=(n_steps + 1,),
        in_specs=in_specs,
        out_specs=o_spec,
        out_shape=jax.ShapeDtypeStruct(x.shape, x.dtype),
        scratch_shapes=[
            pltpu.VMEM((D_MODEL // LANES, POOL_HIST + ts, LANES), jnp.float32),
            pltpu.VMEM((D_MODEL // LANES, CONV_HIST + ts, LANES), jnp.float32),
            pltpu.VMEM((TILES_PER_STEP, ts, D_MODEL), jnp.float32),
            pltpu.VMEM((TILES_PER_STEP, ts, D_MODEL), bf16),
        ] + resident_bf16 + [
            pltpu.VMEM((2 * FFN_CHUNK // LANES, CONV_HIST + ts, LANES), jnp.float32)
            for _ in range(N_FFN_CHUNKS)
        ],
        compiler_params=pltpu.CompilerParams(
            dimension_semantics=("arbitrary",),
            vmem_limit_bytes=V7X_VMEM_LIMIT_BYTES,
        ),
        name="hybrid_block",
    )(*operands)
```

```python
import functools

import jax
import jax.numpy as jnp
from jax import lax
from jax.experimental import pallas as pl
from jax.experimental.pallas import tpu as pltpu

D_MODEL = 1024
N_POOL_GROUPS = 4
POOL_WINDOWS = (2, 4, 8, 16)
POOL_GROUP_DIM = D_MODEL // N_POOL_GROUPS
MXU_TILE = 256
D_FF = 2816
KSIZE = 3
RMS_EPS = 1e-6
LANES = 128
BF16_ROWS = 16

SEQ_TILE = 256
TILES_PER_STEP = 1
POOL_HIST = 16
CONV_HIST = 8
FFN_CHUNK = 256
N_FFN_CHUNKS = D_FF // FFN_CHUNK
UP_AHEAD = 2
CAST_CHUNK_BYTES = 1024 * 1024
CAST_SLOTS = 4
V7X_VMEM_LIMIT_BYTES = 60 * 1024 * 1024


def _rms_norm(x, g):
    inv = lax.rsqrt(jnp.mean(x * x, axis=-1, keepdims=True) + RMS_EPS)
    return x * inv * g


def _dot(a, b):
    return jnp.dot(a, b, preferred_element_type=jnp.float32)


def _slab_store(buf, row0, val):
    for j in range(buf.shape[0]):
        buf[j, row0:row0 + val.shape[0], :] = val[:, j * LANES:(j + 1) * LANES]


def _slab_rows(buf, j, start, rows):
    return buf[pl.ds(j, 1, stride=2), pl.ds(start, rows), :][0]


def _slab_window(buf, start, rows):
    return jnp.concatenate([_slab_rows(buf, j, start, rows) for j in range(buf.shape[0])], axis=-1)


def _carry_history(buf, hist, rows, keep):
    buf[:, 0:hist, :] = jnp.where(keep, buf[:, rows:rows + hist, :], 0.0)


def _causal_taps(buf, hist, rows, w):
    out = None
    for k in range(KSIZE):
        term = w[k:k + 1, :] * _slab_window(buf, hist - (KSIZE - 1) + k, rows)
        out = term if out is None else out + term
    return out


def _block(j):
    return slice(j * MXU_TILE, (j + 1) * MXU_TILE)


def _tile_pair(x_ref, o_ref, x1_buf, h2_buf, t_tok, t_ch, tiles_per_row,
               g_mix_ref, w_in_ref, pool_w_ref, pool_scale_ref, w_pool_proj_ref, conv_w_ref,
               w_conv_out_ref, w_o_ref, g_ffn_ref, w_up_ref, ffn_conv_w_ref, ffn_conv_b_ref,
               w_down_ref, g_final_ref, zp_buf, cv_buf, u_bufs):
    ts = x_ref.shape[0]
    bf16 = jnp.bfloat16
    tok_continues = t_tok + 1 < tiles_per_row
    ch_continues = t_ch + 1 < tiles_per_row

    def gate_val(ref, c):
        gate = ref[:, c * FFN_CHUNK:(c + 1) * FFN_CHUNK]
        val = ref[:, D_FF + c * FFN_CHUNK:D_FF + (c + 1) * FFN_CHUNK]
        return jnp.concatenate([gate, val], axis=1)

    def up_proj(c):
        _slab_store(u_bufs[c], CONV_HIST, _dot(h2_buf[...], gate_val(w_up_ref, c)))

    def ffn_act(c):
        u_buf = u_bufs[c]
        u = (_causal_taps(u_buf, CONV_HIST, ts, gate_val(ffn_conv_w_ref, c))
             + gate_val(ffn_conv_b_ref, c))
        _carry_history(u_buf, CONV_HIST, ts, ch_continues)
        return (jax.nn.silu(u[:, :FFN_CHUNK]) * u[:, FFN_CHUNK:]).astype(bf16)

    def down_proj(c, act):
        return _dot(act, w_down_ref[c * FFN_CHUNK:(c + 1) * FFN_CHUNK, :])

    def in_proj(seg):
        return _dot(v["h"], w_in_ref[:, seg * D_MODEL:(seg + 1) * D_MODEL])

    def blocked_dot(lhs_block, w_ref, blocks):
        out = None
        for j in blocks:
            part = _dot(lhs_block(j), w_ref[_block(j), :])
            out = part if out is None else out + part
        return out

    all_blocks = range(D_MODEL // MXU_TILE)

    def pool_windows():
        pos = t_tok * ts + lax.broadcasted_iota(jnp.int32, (ts, 1), 0) + 1
        slabs_per_group = POOL_GROUP_DIM // LANES
        pooled = []
        for gi, win in enumerate(POOL_WINDOWS):
            def window(back):
                return jnp.concatenate(
                    [_slab_rows(zp_buf, gi * slabs_per_group + j, POOL_HIST - back, ts)
                     for j in range(slabs_per_group)], axis=-1)
            ug = window(0)
            s = ug
            for k in range(1, win):
                s = s + window(k)
            inv_count = 1.0 / jnp.minimum(pos, win).astype(jnp.float32)
            pooled.append((s * inv_count - ug).astype(bf16))
        _carry_history(zp_buf, POOL_HIST, ts, tok_continues)
        return pooled

    v = {}

    def t_conv_c():
        v["z_c"] = in_proj(2)

    def t_windows():
        v["pooled"] = pool_windows()

    def t_conv_v():
        _slab_store(cv_buf, CONV_HIST, v["z_c"] * in_proj(3))

    def t_conv_taps():
        v["conv"] = _causal_taps(cv_buf, CONV_HIST, ts, conv_w_ref[...])
        _carry_history(cv_buf, CONV_HIST, ts, tok_continues)

    def t_pool_groups():
        v["p_groups"] = [
            (_dot(v["pooled"][gi], pool_w_ref[_block(gi), :])
             * pool_scale_ref[:, _block(gi)]).astype(bf16)
            for gi in range(N_POOL_GROUPS)]

    def t_conv_b():
        v["conv_in"] = (in_proj(1) * v["conv"]).astype(bf16)

    def t_pool_proj():
        v["y_pool"] = blocked_dot(lambda j: v["p_groups"][j], w_pool_proj_ref, all_blocks)

    def t_gate_pool():
        v["gated_pool"] = jax.nn.sigmoid(in_proj(4)) * v["y_pool"]

    def t_gate_conv():
        v["gate_conv"] = jax.nn.sigmoid(in_proj(5))

    def t_conv_out():
        y_conv = blocked_dot(lambda j: v["conv_in"][:, _block(j)], w_conv_out_ref, all_blocks)
        v["merged"] = (v["gated_pool"] + v["gate_conv"] * y_conv).astype(bf16)

    def t_out_proj_a():
        v["x1"] = v["x"] + blocked_dot(lambda j: v["merged"][:, _block(j)], w_o_ref, (0, 1))

    def t_out_proj_b():
        x1 = v["x1"] + blocked_dot(lambda j: v["merged"][:, _block(j)], w_o_ref, (2, 3))
        x1_buf[...] = x1
        h2_buf[...] = _rms_norm(x1, g_ffn_ref[...]).astype(bf16)

    def up_head():
        for c in range(UP_AHEAD):
            up_proj(c)

    def token_head():
        v["x"] = x_ref[...]
        v["h"] = _rms_norm(v["x"], g_mix_ref[...]).astype(bf16)
        _slab_store(zp_buf, POOL_HIST, in_proj(0))

    def main():
        token_pieces = (t_conv_c, t_windows, t_conv_v, t_conv_taps, t_pool_groups, t_conv_b,
                        t_pool_proj, t_gate_pool, t_gate_conv, t_conv_out, t_out_proj_a)
        x2 = x1_buf[...]
        for c in range(N_FFN_CHUNKS):
            act = ffn_act(c)
            if c + UP_AHEAD < N_FFN_CHUNKS:
                up_proj(c + UP_AHEAD)
            if token_pieces[c] is not None:
                token_pieces[c]()
            x2 = x2 + down_proj(c, act)
        v["x2"] = x2

    def tail():
        o_ref[...] = _rms_norm(v["x2"], g_final_ref[...])
        t_out_proj_b()

    return up_head, token_head, main, tail


def _cast_chunk_rows(rows, cols):
    budget = max(BF16_ROWS, CAST_CHUNK_BYTES // (4 * cols))
    return max(r for r in range(BF16_ROWS, min(rows, budget) + 1, BF16_ROWS) if rows % r == 0)


def _load_as_bf16(src_hbm, dst, chunk_rows):
    rows, cols = src_hbm.shape
    assert rows % chunk_rows == 0 and dst.shape == src_hbm.shape
    n_chunks = rows // chunk_rows
    ahead = CAST_SLOTS - 1

    def scoped(stage, sems):
        def chunk_copy(i, slot):
            return pltpu.make_async_copy(
                src_hbm.at[pl.ds(i * chunk_rows, chunk_rows), :], stage.at[slot], sems.at[slot])

        for k in range(min(ahead, n_chunks)):
            chunk_copy(k, k).start()

        def body(i, carry):
            @pl.when(i + ahead < n_chunks)
            def _():
                chunk_copy(i + ahead, lax.rem(i + ahead, CAST_SLOTS)).start()

            slot = lax.rem(i, CAST_SLOTS)
            chunk_copy(i, slot).wait()
            row0 = pl.multiple_of(i * chunk_rows, chunk_rows)
            dst[pl.ds(row0, chunk_rows), :] = stage[slot].astype(jnp.bfloat16)
            return carry

        lax.fori_loop(0, n_chunks, body, 0)

    pl.run_scoped(scoped, pltpu.VMEM((CAST_SLOTS, chunk_rows, cols), jnp.float32),
                  pltpu.SemaphoreType.DMA((CAST_SLOTS,)))


def _block_kernel(tiles_per_row, x_ref, *refs):
    (g_mix_ref, w_in_hbm, pool_w_hbm, pool_scale_ref, w_pool_proj_hbm, conv_w_ref,
     w_conv_out_hbm, w_o_hbm, g_ffn_ref, w_up_hbm, ffn_conv_w_ref, ffn_conv_b_ref, w_down_hbm,
     g_final_ref, o_ref, zp_buf, cv_buf, x1_buf, h2_buf,
     w_in_ref, pool_w_ref, w_pool_proj_ref, w_conv_out_ref, w_o_ref, w_up_ref,
     w_down_ref) = refs[:26]
    u_bufs = refs[26:]
    weights = (g_mix_ref, w_in_ref, pool_w_ref, pool_scale_ref, w_pool_proj_ref, conv_w_ref,
               w_conv_out_ref, w_o_ref, g_ffn_ref, w_up_ref, ffn_conv_w_ref, ffn_conv_b_ref,
               w_down_ref, g_final_ref)
    ts = x_ref.shape[0] // TILES_PER_STEP
    step = pl.program_id(0)
    n_steps = pl.num_programs(0) - 1
    first_tok = jnp.minimum(step, n_steps - 1) * TILES_PER_STEP
    first_ch = jnp.maximum(step - 1, 0) * TILES_PER_STEP

    @pl.when(step == 0)
    def _():
        zp_buf[:, 0:POOL_HIST, :] = jnp.zeros((zp_buf.shape[0], POOL_HIST, LANES), jnp.float32)
        for buf in (cv_buf,) + tuple(u_bufs):
            buf[:, 0:CONV_HIST, :] = jnp.zeros((buf.shape[0], CONV_HIST, LANES), jnp.float32)
        x1_buf[...] = jnp.zeros(x1_buf.shape, jnp.float32)
        h2_buf[...] = jnp.zeros(h2_buf.shape, jnp.bfloat16)
        for src, dst in ((w_up_hbm, w_up_ref), (w_in_hbm, w_in_ref), (w_down_hbm, w_down_ref),
                         (pool_w_hbm, pool_w_ref), (w_pool_proj_hbm, w_pool_proj_ref),
                         (w_conv_out_hbm, w_conv_out_ref), (w_o_hbm, w_o_ref)):
            _load_as_bf16(src, dst, _cast_chunk_rows(*src.shape))

    pairs = []
    for s in range(TILES_PER_STEP):
        rows = pl.ds(s * ts, ts)
        pairs.append(_tile_pair(
            x_ref.at[rows], o_ref.at[rows], x1_buf.at[s], h2_buf.at[s],
            lax.rem(first_tok + s, tiles_per_row), lax.rem(first_ch + s, tiles_per_row),
            tiles_per_row, *weights, zp_buf, cv_buf, u_bufs))
    up_head, token_head, _, _ = pairs[0]
    up_head()
    token_head()
    for s, (_, _, main, tail) in enumerate(pairs):
        main()
        if s + 1 < TILES_PER_STEP:
            pairs[s + 1][0]()
        tail()
        if s + 1 < TILES_PER_STEP:
            pairs[s + 1][1]()


def _resident(shape):
    return pl.BlockSpec(shape, lambda i: (0,) * len(shape), pipeline_mode=pl.Buffered(1))


@jax.jit
def kernel(x, norm_mix, w_in, pool_w, pool_scale, w_pool_proj, conv_w, w_conv_out, w_o,
           norm_ffn, w_up, ffn_conv_w, ffn_conv_b, w_down, norm_final):
    batch, seq, d_model = x.shape
    ts = SEQ_TILE
    step_rows = ts * TILES_PER_STEP
    assert d_model == D_MODEL and seq % step_rows == 0 and D_FF % FFN_CHUNK == 0
    assert POOL_GROUP_DIM == MXU_TILE and N_FFN_CHUNKS == 11
    assert norm_mix.shape[0] == 1, "single-layer block"
    bf16 = jnp.bfloat16
    tiles_per_row = seq // ts
    steps_per_row = seq // step_rows
    n_steps = batch * steps_per_row

    matmul_weights = {2, 3, 5, 7, 8, 10, 13}
    operands = (
        x,
        norm_mix[0][None, :],
        w_in[0],
        pool_w[0].reshape(N_POOL_GROUPS * POOL_GROUP_DIM, POOL_GROUP_DIM),
        pool_scale[0][None, :],
        w_pool_proj[0],
        conv_w[0],
        w_conv_out[0],
        w_o[0],
        norm_ffn[0][None, :],
        w_up[0],
        ffn_conv_w[0],
        ffn_conv_b[0][None, :],
        w_down[0],
        norm_final[None, :],
    )

    def block_index(blk):
        return blk // steps_per_row, blk % steps_per_row, 0

    x_spec = pl.BlockSpec((None, step_rows, D_MODEL), lambda i: block_index(jnp.minimum(i, n_steps - 1)))
    o_spec = pl.BlockSpec((None, step_rows, D_MODEL), lambda i: block_index(jnp.maximum(i - 1, 0)))
    in_specs = [x_spec] + [
        pl.BlockSpec(memory_space=pl.ANY) if k in matmul_weights else _resident(op.shape)
        for k, op in enumerate(operands) if k > 0]
    resident_bf16 = [pltpu.VMEM(operands[k].shape, bf16) for k in sorted(matmul_weights)]

    return pl.pallas_call(
        functools.partial(_block_kernel, tiles_per_row),
        grid=(n_steps + 1,),
        in_specs=in_specs,
        out_specs=o_spec,
        out_shape=jax.ShapeDtypeStruct(x.shape, x.dtype),
        scratch_shapes=[
            pltpu.VMEM((D_MODEL // LANES, POOL_HIST + ts, LANES), jnp.float32),
            pltpu.VMEM((D_MODEL // LANES, CONV_HIST + ts, LANES), jnp.float32),
            pltpu.VMEM((TILES_PER_STEP, ts, D_MODEL), jnp.float32),
            pltpu.VMEM((TILES_PER_STEP, ts, D_MODEL), bf16),
        ] + resident_bf16 + [
            pltpu.VMEM((2 * FFN_CHUNK // LANES, CONV_HIST + ts, LANES), jnp.float32)
            for _ in range(N_FFN_CHUNKS)
        ],
        compiler_params=pltpu.CompilerParams(
            dimension_semantics=("arbitrary",),
            vmem_limit_bytes=V7X_VMEM_LIMIT_BYTES,
        ),
        name="hybrid_block",
    )(*operands)
```

```python
import functools

import jax
import jax.numpy as jnp
from jax import lax
from jax.experimental import pallas as pl
from jax.experimental.pallas import tpu as pltpu

D_MODEL = 1024
N_POOL_GROUPS = 4
POOL_WINDOWS = (2, 4, 8, 16)
POOL_GROUP_DIM = D_MODEL // N_POOL_GROUPS
MXU_TILE = 256
D_FF = 2816
KSIZE = 3
RMS_EPS = 1e-6
LANES = 128
BF16_ROWS = 16

SEQ_TILE = 256
POOL_HIST = 16
CONV_HIST = 8
FFN_CHUNK = 256
N_FFN_CHUNKS = D_FF // FFN_CHUNK
UP_AHEAD = 2
CAST_CHUNK_BYTES = 1024 * 1024
CAST_SLOTS = 4
V7X_VMEM_LIMIT_BYTES = 60 * 1024 * 1024


def _rms_norm(x, g):
    inv = lax.rsqrt(jnp.mean(x * x, axis=-1, keepdims=True) + RMS_EPS)
    return x * inv * g


def _dot(a, b):
    return jnp.dot(a, b, preferred_element_type=jnp.float32)


def _slab_store(buf, row0, val):
    for j in range(buf.shape[0]):
        buf[j, row0:row0 + val.shape[0], :] = val[:, j * LANES:(j + 1) * LANES]


def _slab_rows(buf, j, start, rows):
    return buf[pl.ds(j, 1, stride=2), pl.ds(start, rows), :][0]


def _slab_window(buf, start, rows):
    return jnp.concatenate([_slab_rows(buf, j, start, rows) for j in range(buf.shape[0])], axis=-1)


def _carry_history(buf, hist, rows, keep):
    buf[:, 0:hist, :] = jnp.where(keep, buf[:, rows:rows + hist, :], 0.0)


def _causal_taps(buf, hist, rows, w):
    out = None
    for k in range(KSIZE):
        term = w[k:k + 1, :] * _slab_window(buf, hist - (KSIZE - 1) + k, rows)
        out = term if out is None else out + term
    return out


def _block(j):
    return slice(j * MXU_TILE, (j + 1) * MXU_TILE)


def _tile_pair(x_ref, o_ref, x1_buf, h2_buf, t_tok, t_ch, tiles_per_row,
               g_mix_ref, w_in_ref, pool_w_ref, pool_scale_ref, w_pool_proj_ref, conv_w_ref,
               w_conv_out_ref, w_o_ref, g_ffn_ref, w_up_ref, ffn_conv_w_ref, ffn_conv_b_ref,
               w_down_ref, g_final_ref, zp_buf, cv_buf, u_bufs):
    ts = x_ref.shape[0]
    bf16 = jnp.bfloat16
    tok_continues = t_tok + 1 < tiles_per_row
    ch_continues = t_ch + 1 < tiles_per_row

    def gate_val(ref, c):
        gate = ref[:, c * FFN_CHUNK:(c + 1) * FFN_CHUNK]
        val = ref[:, D_FF + c * FFN_CHUNK:D_FF + (c + 1) * FFN_CHUNK]
        return jnp.concatenate([gate, val], axis=1)

    def up_proj(c):
        _slab_store(u_bufs[c], CONV_HIST, _dot(h2_buf[...], gate_val(w_up_ref, c)))

    def ffn_act(c):
        u_buf = u_bufs[c]
        u = (_causal_taps(u_buf, CONV_HIST, ts, gate_val(ffn_conv_w_ref, c))
             + gate_val(ffn_conv_b_ref, c))
        _carry_history(u_buf, CONV_HIST, ts, ch_continues)
        return (jax.nn.silu(u[:, :FFN_CHUNK]) * u[:, FFN_CHUNK:]).astype(bf16)

    def down_proj(c, act):
        return _dot(act, w_down_ref[c * FFN_CHUNK:(c + 1) * FFN_CHUNK, :])

    def in_proj(seg):
        return _dot(v["h"], w_in_ref[:, seg * D_MODEL:(seg + 1) * D_MODEL])

    def blocked_dot(lhs_block, w_ref, blocks):
        out = None
        for j in blocks:
            part = _dot(lhs_block(j), w_ref[_block(j), :])
            out = part if out is None else out + part
        return out

    all_blocks = range(D_MODEL // MXU_TILE)

    def pool_windows():
        pos = t_tok * ts + lax.broadcasted_iota(jnp.int32, (ts, 1), 0) + 1
        slabs_per_group = POOL_GROUP_DIM // LANES
        pooled = []
        for gi, win in enumerate(POOL_WINDOWS):
            def window(back):
                return jnp.concatenate(
                    [_slab_rows(zp_buf, gi * slabs_per_group + j, POOL_HIST - back, ts)
                     for j in range(slabs_per_group)], axis=-1)
            ug = window(0)
            s = ug
            for k in range(1, win):
                s = s + window(k)
            inv_count = 1.0 / jnp.minimum(pos, win).astype(jnp.float32)
            pooled.append((s * inv_count - ug).astype(bf16))
        _carry_history(zp_buf, POOL_HIST, ts, tok_continues)
        return pooled

    v = {}

    def t_conv_c():
        v["z_c"] = in_proj(2)

    def t_windows():
        v["pooled"] = pool_windows()

    def t_conv_v():
        _slab_store(cv_buf, CONV_HIST, v["z_c"] * in_proj(3))

    def t_conv_taps():
        v["conv"] = _causal_taps(cv_buf, CONV_HIST, ts, conv_w_ref[...])
        _carry_history(cv_buf, CONV_HIST, ts, tok_continues)

    def t_pool_groups():
        v["p_groups"] = [
            (_dot(v["pooled"][gi], pool_w_ref[_block(gi), :])
             * pool_scale_ref[:, _block(gi)]).astype(bf16)
            for gi in range(N_POOL_GROUPS)]

    def t_conv_b():
        v["conv_in"] = (in_proj(1) * v["conv"]).astype(bf16)

    def t_pool_proj():
        v["y_pool"] = blocked_dot(lambda j: v["p_groups"][j], w_pool_proj_ref, all_blocks)

    def t_gate_pool():
        v["gated_pool"] = jax.nn.sigmoid(in_proj(4)) * v["y_pool"]

    def t_gate_conv():
        v["gate_conv"] = jax.nn.sigmoid(in_proj(5))

    def t_conv_out():
        y_conv = blocked_dot(lambda j: v["conv_in"][:, _block(j)], w_conv_out_ref, all_blocks)
        v["merged"] = (v["gated_pool"] + v["gate_conv"] * y_conv).astype(bf16)

    def t_out_proj_a():
        v["x1"] = v["x"] + blocked_dot(lambda j: v["merged"][:, _block(j)], w_o_ref, (0, 1))

    def t_out_proj_b():
        x1 = v["x1"] + blocked_dot(lambda j: v["merged"][:, _block(j)], w_o_ref, (2, 3))
        x1_buf[...] = x1
        h2_buf[...] = _rms_norm(x1, g_ffn_ref[...]).astype(bf16)

    def up_head():
        for c in range(UP_AHEAD):
            up_proj(c)

    def token_head():
        v["x"] = x_ref[...]
        v["h"] = _rms_norm(v["x"], g_mix_ref[...]).astype(bf16)
        _slab_store(zp_buf, POOL_HIST, in_proj(0))

    def main():
        token_pieces = (t_conv_c, t_windows, t_conv_v, t_conv_taps, t_pool_groups, t_conv_b,
                        t_pool_proj, t_gate_pool, t_gate_conv, t_conv_out, t_out_proj_a)
        x2 = x1_buf[...]
        for c in range(N_FFN_CHUNKS):
            act = ffn_act(c)
            if c + UP_AHEAD < N_FFN_CHUNKS:
                up_proj(c + UP_AHEAD)
            if token_pieces[c] is not None:
                token_pieces[c]()
            x2 = x2 + down_proj(c, act)
        v["x2"] = x2

    def tail():
        o_ref[...] = _rms_norm(v["x2"], g_final_ref[...])
        t_out_proj_b()

    return up_head, token_head, main, tail


def _cast_chunk_rows(rows, cols):
    budget = max(BF16_ROWS, CAST_CHUNK_BYTES // (4 * cols))
    return max(r for r in range(BF16_ROWS, min(rows, budget) + 1, BF16_ROWS) if rows % r == 0)


def _load_as_bf16(src_hbm, dst, chunk_rows):
    rows, cols = src_hbm.shape
    assert rows % chunk_rows == 0 and dst.shape == src_hbm.shape
    n_chunks = rows // chunk_rows
    ahead = CAST_SLOTS - 1

    def scoped(stage, sems):
        def chunk_copy(i, slot):
            return pltpu.make_async_copy(
                src_hbm.at[pl.ds(i * chunk_rows, chunk_rows), :], stage.at[slot], sems.at[slot])

        for k in range(min(ahead, n_chunks)):
            chunk_copy(k, k).start()

        def body(i, carry):
            @pl.when(i + ahead < n_chunks)
            def _():
                chunk_copy(i + ahead, lax.rem(i + ahead, CAST_SLOTS)).start()

            slot = lax.rem(i, CAST_SLOTS)
            chunk_copy(i, slot).wait()
            row0 = pl.multiple_of(i * chunk_rows, chunk_rows)
            dst[pl.ds(row0, chunk_rows), :] = stage[slot].astype(jnp.bfloat16)
            return carry

        lax.fori_loop(0, n_chunks, body, 0)

    pl.run_scoped(scoped, pltpu.VMEM((CAST_SLOTS, chunk_rows, cols), jnp.float32),
                  pltpu.SemaphoreType.DMA((CAST_SLOTS,)))


def _block_kernel(tiles_per_row, n_tiles, x_hbm, *refs):
    (g_mix_ref, w_in_hbm, pool_w_hbm, pool_scale_ref, w_pool_proj_hbm, conv_w_ref,
     w_conv_out_hbm, w_o_hbm, g_ffn_ref, w_up_hbm, ffn_conv_w_ref, ffn_conv_b_ref, w_down_hbm,
     g_final_ref, o_hbm, x_stage, o_stage, io_sems, zp_buf, cv_buf, x1_buf, h2_buf,
     w_in_ref, pool_w_ref, w_pool_proj_ref, w_conv_out_ref, w_o_ref, w_up_ref,
     w_down_ref) = refs[:29]
    u_bufs = refs[29:]
    weights = (g_mix_ref, w_in_ref, pool_w_ref, pool_scale_ref, w_pool_proj_ref, conv_w_ref,
               w_conv_out_ref, w_o_ref, g_ffn_ref, w_up_ref, ffn_conv_w_ref, ffn_conv_b_ref,
               w_down_ref, g_final_ref)
    ts = x_stage.shape[1]

    def tile_of(ref, tile):
        row0 = pl.multiple_of(lax.rem(tile, tiles_per_row) * ts, ts)
        return ref.at[tile // tiles_per_row, pl.ds(row0, ts), :]

    def x_copy(tile, slot):
        return pltpu.make_async_copy(tile_of(x_hbm, tile), x_stage.at[slot], io_sems.at[0, slot])

    def o_copy(tile, slot):
        return pltpu.make_async_copy(o_stage.at[slot], tile_of(o_hbm, tile), io_sems.at[1, slot])

    x_copy(0, 0).start()
    zp_buf[:, 0:POOL_HIST, :] = jnp.zeros((zp_buf.shape[0], POOL_HIST, LANES), jnp.float32)
    for buf in (cv_buf,) + tuple(u_bufs):
        buf[:, 0:CONV_HIST, :] = jnp.zeros((buf.shape[0], CONV_HIST, LANES), jnp.float32)
    x1_buf[...] = jnp.zeros(x1_buf.shape, jnp.float32)
    h2_buf[...] = jnp.zeros(h2_buf.shape, jnp.bfloat16)
    for src, dst in ((w_up_hbm, w_up_ref), (w_in_hbm, w_in_ref), (w_down_hbm, w_down_ref),
                     (pool_w_hbm, pool_w_ref), (w_pool_proj_hbm, w_pool_proj_ref),
                     (w_conv_out_hbm, w_conv_out_ref), (w_o_hbm, w_o_ref)):
        _load_as_bf16(src, dst, _cast_chunk_rows(*src.shape))

    def step(i, carry):
        slot = lax.rem(i, 2)

        @pl.when(i + 1 < n_tiles)
        def _():
            x_copy(i + 1, 1 - slot).start()

        @pl.when(i < n_tiles)
        def _():
            x_copy(i, slot).wait()

        @pl.when(i >= 3)
        def _():
            o_copy(i - 3, slot).wait()

        t_tok = lax.rem(jnp.minimum(i, n_tiles - 1), tiles_per_row)
        t_ch = lax.rem(jnp.maximum(i - 1, 0), tiles_per_row)
        up_head, token_head, main, tail = _tile_pair(
            x_stage.at[slot], o_stage.at[slot], x1_buf, h2_buf, t_tok, t_ch, tiles_per_row,
            *weights, zp_buf, cv_buf, u_bufs)
        up_head()
        token_head()
        main()
        tail()

        @pl.when(i >= 1)
        def _():
            o_copy(i - 1, slot).start()

        return carry

    lax.fori_loop(0, n_tiles + 1, step, 0)
    o_copy(n_tiles - 2, (n_tiles - 1) % 2).wait()
    o_copy(n_tiles - 1, n_tiles % 2).wait()


def _whole(shape):
    return pl.BlockSpec(shape, lambda i: (0,) * len(shape))


@jax.jit
def kernel(x, norm_mix, w_in, pool_w, pool_scale, w_pool_proj, conv_w, w_conv_out, w_o,
           norm_ffn, w_up, ffn_conv_w, ffn_conv_b, w_down, norm_final):
    batch, seq, d_model = x.shape
    ts = SEQ_TILE
    assert d_model == D_MODEL and seq % ts == 0 and D_FF % FFN_CHUNK == 0
    assert POOL_GROUP_DIM == MXU_TILE and N_FFN_CHUNKS == 11
    assert norm_mix.shape[0] == 1, "single-layer block"
    bf16 = jnp.bfloat16
    tiles_per_row = seq // ts
    n_tiles = batch * tiles_per_row
    assert n_tiles >= 2

    matmul_weights = {2, 3, 5, 7, 8, 10, 13}
    operands = (
        x,
        norm_mix[0][None, :],
        w_in[0],
        pool_w[0].reshape(N_POOL_GROUPS * POOL_GROUP_DIM, POOL_GROUP_DIM),
        pool_scale[0][None, :],
        w_pool_proj[0],
        conv_w[0],
        w_conv_out[0],
        w_o[0],
        norm_ffn[0][None, :],
        w_up[0],
        ffn_conv_w[0],
        ffn_conv_b[0][None, :],
        w_down[0],
        norm_final[None, :],
    )

    any_spec = pl.BlockSpec(memory_space=pl.ANY)
    in_specs = [any_spec] + [
        any_spec if k in matmul_weights else _whole(op.shape)
        for k, op in enumerate(operands) if k > 0]
    resident_bf16 = [pltpu.VMEM(operands[k].shape, bf16) for k in sorted(matmul_weights)]

    return pl.pallas_call(
        functools.partial(_block_kernel, tiles_per_row, n_tiles),
        grid=(1,),
        in_specs=in_specs,
        out_specs=any_spec,
        out_shape=jax.ShapeDtypeStruct(x.shape, x.dtype),
        scratch_shapes=[
            pltpu.VMEM((2, ts, D_MODEL), jnp.float32),
            pltpu.VMEM((2, ts, D_MODEL), jnp.float32),
            pltpu.SemaphoreType.DMA((2, 2)),
            pltpu.VMEM((D_MODEL // LANES, POOL_HIST + ts, LANES), jnp.float32),
            pltpu.VMEM((D_MODEL // LANES, CONV_HIST + ts, LANES), jnp.float32),
            pltpu.VMEM((ts, D_MODEL), jnp.float32),
            pltpu.VMEM((ts, D_MODEL), bf16),
        ] + resident_bf16 + [
            pltpu.VMEM((2 * FFN_CHUNK // LANES, CONV_HIST + ts, LANES), jnp.float32)
            for _ in range(N_FFN_CHUNKS)
        ],
        compiler_params=pltpu.CompilerParams(
            dimension_semantics=("arbitrary",),
            vmem_limit_bytes=V7X_VMEM_LIMIT_BYTES,
        ),
        name="hybrid_block",
    )(*operands)
```

```python
import functools

import jax
import jax.numpy as jnp
from jax import lax
from jax.experimental import pallas as pl
from jax.experimental.pallas import tpu as pltpu

D_MODEL = 1024
N_POOL_GROUPS = 4
POOL_WINDOWS = (2, 4, 8, 16)
POOL_GROUP_DIM = D_MODEL // N_POOL_GROUPS
MXU_TILE = 256
D_FF = 2816
KSIZE = 3
RMS_EPS = 1e-6
LANES = 128
BF16_ROWS = 16

SEQ_TILE = 256
POOL_HIST = 16
CONV_HIST = 8
FFN_CHUNK = 256
N_FFN_CHUNKS = D_FF // FFN_CHUNK
UP_AHEAD = 2
CAST_CHUNK_BYTES = 1024 * 1024
CAST_SLOTS = 6
V7X_VMEM_LIMIT_BYTES = 60 * 1024 * 1024


def _rms_norm(x, g):
    inv = lax.rsqrt(jnp.mean(x * x, axis=-1, keepdims=True) + RMS_EPS)
    return x * inv * g


def _dot(a, b):
    return jnp.dot(a, b, preferred_element_type=jnp.float32)


def _slab_store(buf, row0, val):
    for j in range(buf.shape[0]):
        buf[j, row0:row0 + val.shape[0], :] = val[:, j * LANES:(j + 1) * LANES]


def _slab_rows(buf, j, start, rows):
    return buf[pl.ds(j, 1, stride=2), pl.ds(start, rows), :][0]


def _slab_window(buf, start, rows):
    return jnp.concatenate([_slab_rows(buf, j, start, rows) for j in range(buf.shape[0])], axis=-1)


def _carry_history(buf, hist, rows, keep):
    buf[:, 0:hist, :] = jnp.where(keep, buf[:, rows:rows + hist, :], 0.0)


def _causal_taps(buf, hist, rows, w):
    out = None
    for k in range(KSIZE):
        term = w[k:k + 1, :] * _slab_window(buf, hist - (KSIZE - 1) + k, rows)
        out = term if out is None else out + term
    return out


def _block(j):
    return slice(j * MXU_TILE, (j + 1) * MXU_TILE)


def _tile_pair(x_ref, o_ref, x1_buf, h2_buf, t_tok, t_ch, tiles_per_row,
               g_mix_ref, w_in_ref, w_pool_ref, conv_w_ref,
               w_conv_out_ref, w_o_ref, g_ffn_ref, w_up_ref, ffn_conv_w_ref, ffn_conv_b_ref,
               w_down_ref, g_final_ref, zp_buf, cv_buf, u_bufs):
    ts = x_ref.shape[0]
    bf16 = jnp.bfloat16
    tok_continues = t_tok + 1 < tiles_per_row
    ch_continues = t_ch + 1 < tiles_per_row

    def gate_val(ref, c):
        gate = ref[:, c * FFN_CHUNK:(c + 1) * FFN_CHUNK]
        val = ref[:, D_FF + c * FFN_CHUNK:D_FF + (c + 1) * FFN_CHUNK]
        return jnp.concatenate([gate, val], axis=1)

    def up_proj(c):
        _slab_store(u_bufs[c], CONV_HIST, _dot(h2_buf[...], gate_val(w_up_ref, c)))

    def ffn_act(c):
        u_buf = u_bufs[c]
        u = (_causal_taps(u_buf, CONV_HIST, ts, gate_val(ffn_conv_w_ref, c))
             + gate_val(ffn_conv_b_ref, c))
        _carry_history(u_buf, CONV_HIST, ts, ch_continues)
        return (jax.nn.silu(u[:, :FFN_CHUNK]) * u[:, FFN_CHUNK:]).astype(bf16)

    def down_proj(c, act):
        return _dot(act, w_down_ref[c * FFN_CHUNK:(c + 1) * FFN_CHUNK, :])

    def in_proj(seg):
        return _dot(v["h"], w_in_ref[:, seg * D_MODEL:(seg + 1) * D_MODEL])

    def blocked_dot(lhs_block, w_ref, blocks):
        out = None
        for j in blocks:
            part = _dot(lhs_block(j), w_ref[_block(j), :])
            out = part if out is None else out + part
        return out

    all_blocks = range(D_MODEL // MXU_TILE)

    def pool_windows():
        pos = t_tok * ts + lax.broadcasted_iota(jnp.int32, (ts, 1), 0) + 1
        slabs_per_group = POOL_GROUP_DIM // LANES
        pooled = []
        for gi, win in enumerate(POOL_WINDOWS):
            def window(back):
                return jnp.concatenate(
                    [_slab_rows(zp_buf, gi * slabs_per_group + j, POOL_HIST - back, ts)
                     for j in range(slabs_per_group)], axis=-1)
            ug = window(0)
            s = ug
            for k in range(1, win):
                s = s + window(k)
            inv_count = 1.0 / jnp.minimum(pos, win).astype(jnp.float32)
            pooled.append((s * inv_count - ug).astype(bf16))
        _carry_history(zp_buf, POOL_HIST, ts, tok_continues)
        return pooled

    v = {}

    def t_conv_c():
        v["z_c"] = in_proj(2)

    def t_windows():
        v["pooled"] = pool_windows()

    def t_conv_v():
        _slab_store(cv_buf, CONV_HIST, v["z_c"] * in_proj(3))

    def t_conv_taps():
        v["conv"] = _causal_taps(cv_buf, CONV_HIST, ts, conv_w_ref[...])
        _carry_history(cv_buf, CONV_HIST, ts, tok_continues)

    def t_conv_b():
        v["conv_in"] = (in_proj(1) * v["conv"]).astype(bf16)

    def t_pool_proj():
        v["y_pool"] = blocked_dot(lambda j: v["pooled"][j], w_pool_ref, all_blocks)

    def t_gate_pool():
        v["gated_pool"] = jax.nn.sigmoid(in_proj(4)) * v["y_pool"]

    def t_gate_conv():
        v["gate_conv"] = jax.nn.sigmoid(in_proj(5))

    def t_conv_out():
        y_conv = blocked_dot(lambda j: v["conv_in"][:, _block(j)], w_conv_out_ref, all_blocks)
        v["merged"] = (v["gated_pool"] + v["gate_conv"] * y_conv).astype(bf16)

    def t_out_proj_a():
        v["x1"] = v["x"] + blocked_dot(lambda j: v["merged"][:, _block(j)], w_o_ref, (0, 1))

    def t_out_proj_b():
        x1 = v["x1"] + blocked_dot(lambda j: v["merged"][:, _block(j)], w_o_ref, (2, 3))
        x1_buf[...] = x1
        h2_buf[...] = _rms_norm(x1, g_ffn_ref[...]).astype(bf16)

    def up_head():
        for c in range(UP_AHEAD):
            up_proj(c)

    def token_head():
        v["x"] = x_ref[...]
        v["h"] = _rms_norm(v["x"], g_mix_ref[...]).astype(bf16)
        _slab_store(zp_buf, POOL_HIST, in_proj(0))

    def main():
        token_pieces = (t_conv_c, t_windows, t_conv_v, t_conv_taps, None, t_conv_b,
                        t_pool_proj, t_gate_pool, t_gate_conv, t_conv_out, t_out_proj_a)
        x2 = x1_buf[...]
        for c in range(N_FFN_CHUNKS):
            act = ffn_act(c)
            if c + UP_AHEAD < N_FFN_CHUNKS:
                up_proj(c + UP_AHEAD)
            if token_pieces[c] is not None:
                token_pieces[c]()
            x2 = x2 + down_proj(c, act)
        v["x2"] = x2

    def tail():
        o_ref[...] = _rms_norm(v["x2"], g_final_ref[...])
        t_out_proj_b()

    def pair():
        up_head()
        token_head()
        main()
        tail()

    def token_only():
        token_head()
        for piece in (t_conv_c, t_windows, t_conv_v, t_conv_taps, t_conv_b, t_pool_proj,
                      t_gate_pool, t_gate_conv, t_conv_out, t_out_proj_a, t_out_proj_b):
            piece()

    def channel_only():
        up_head()
        x2 = x1_buf[...]
        for c in range(N_FFN_CHUNKS):
            act = ffn_act(c)
            if c + UP_AHEAD < N_FFN_CHUNKS:
                up_proj(c + UP_AHEAD)
            x2 = x2 + down_proj(c, act)
        o_ref[...] = _rms_norm(x2, g_final_ref[...])

    return pair, token_only, channel_only


def _cast_chunk_rows(rows, cols):
    budget = max(BF16_ROWS, CAST_CHUNK_BYTES // (4 * cols))
    return max(r for r in range(BF16_ROWS, min(rows, budget) + 1, BF16_ROWS) if rows % r == 0)


def _load_as_bf16(src_hbm, dst, chunk_rows):
    rows, cols = src_hbm.shape
    assert rows % chunk_rows == 0 and dst.shape == src_hbm.shape
    n_chunks = rows // chunk_rows
    ahead = CAST_SLOTS - 1

    def scoped(stage, sems):
        def chunk_copy(i, slot):
            return pltpu.make_async_copy(
                src_hbm.at[pl.ds(i * chunk_rows, chunk_rows), :], stage.at[slot], sems.at[slot])

        for k in range(min(ahead, n_chunks)):
            chunk_copy(k, k).start()

        def body(i, carry):
            @pl.when(i + ahead < n_chunks)
            def _():
                chunk_copy(i + ahead, lax.rem(i + ahead, CAST_SLOTS)).start()

            slot = lax.rem(i, CAST_SLOTS)
            chunk_copy(i, slot).wait()
            row0 = pl.multiple_of(i * chunk_rows, chunk_rows)
            dst[pl.ds(row0, chunk_rows), :] = stage[slot].astype(jnp.bfloat16)
            return carry

        lax.fori_loop(0, n_chunks, body, 0)

    pl.run_scoped(scoped, pltpu.VMEM((CAST_SLOTS, chunk_rows, cols), jnp.float32),
                  pltpu.SemaphoreType.DMA((CAST_SLOTS,)))


def _fold_pool_weights(pool_w_hbm, pool_scale_ref, w_pool_proj_hbm, dst):
    def scoped(pool_w, w_pool_proj, sems):
        copies = (pltpu.make_async_copy(pool_w_hbm, pool_w, sems.at[0]),
                  pltpu.make_async_copy(w_pool_proj_hbm, w_pool_proj, sems.at[1]))
        for copy in copies:
            copy.start()
        for copy in copies:
            copy.wait()
        for gi in range(N_POOL_GROUPS):
            scaled = pool_w[_block(gi), :] * pool_scale_ref[:, _block(gi)]
            dst[_block(gi), :] = jnp.dot(
                scaled, w_pool_proj[_block(gi), :], precision=lax.Precision.HIGHEST,
                preferred_element_type=jnp.float32).astype(jnp.bfloat16)

    pl.run_scoped(scoped, pltpu.VMEM(pool_w_hbm.shape, jnp.float32),
                  pltpu.VMEM(w_pool_proj_hbm.shape, jnp.float32), pltpu.SemaphoreType.DMA((2,)))


def _block_kernel(tiles_per_row, n_tiles, x_hbm, *refs):
    (g_mix_ref, w_in_hbm, pool_w_hbm, pool_scale_ref, w_pool_proj_hbm, conv_w_ref,
     w_conv_out_hbm, w_o_hbm, g_ffn_ref, w_up_hbm, ffn_conv_w_ref, ffn_conv_b_ref, w_down_hbm,
     g_final_ref, o_hbm, x_stage, o_stage, io_sems, zp_buf, cv_buf, x1_buf, h2_buf,
     w_in_ref, w_pool_ref, w_conv_out_ref, w_o_ref, w_up_ref, w_down_ref) = refs[:28]
    u_bufs = refs[28:]
    weights = (g_mix_ref, w_in_ref, w_pool_ref, conv_w_ref,
               w_conv_out_ref, w_o_ref, g_ffn_ref, w_up_ref, ffn_conv_w_ref, ffn_conv_b_ref,
               w_down_ref, g_final_ref)
    ts = x_stage.shape[1]

    def tile_of(ref, tile):
        row0 = pl.multiple_of(lax.rem(tile, tiles_per_row) * ts, ts)
        return ref.at[tile // tiles_per_row, pl.ds(row0, ts), :]

    def x_copy(tile, slot):
        return pltpu.make_async_copy(tile_of(x_hbm, tile), x_stage.at[slot], io_sems.at[0, slot])

    def o_copy(tile, slot):
        return pltpu.make_async_copy(o_stage.at[slot], tile_of(o_hbm, tile), io_sems.at[1, slot])

    x_copy(0, 0).start()
    zp_buf[:, 0:POOL_HIST, :] = jnp.zeros((zp_buf.shape[0], POOL_HIST, LANES), jnp.float32)
    for buf in (cv_buf,) + tuple(u_bufs):
        buf[:, 0:CONV_HIST, :] = jnp.zeros((buf.shape[0], CONV_HIST, LANES), jnp.float32)
    for src, dst in ((w_in_hbm, w_in_ref), (w_conv_out_hbm, w_conv_out_ref), (w_o_hbm, w_o_ref),
                     (w_up_hbm, w_up_ref), (w_down_hbm, w_down_ref)):
        _load_as_bf16(src, dst, _cast_chunk_rows(*src.shape))
    _fold_pool_weights(pool_w_hbm, pool_scale_ref, w_pool_proj_hbm, w_pool_ref)

    def phases(slot, tok_tile, ch_tile):
        return _tile_pair(
            x_stage.at[slot], o_stage.at[slot], x1_buf, h2_buf,
            lax.rem(tok_tile, tiles_per_row), lax.rem(ch_tile, tiles_per_row), tiles_per_row,
            *weights, zp_buf, cv_buf, u_bufs)

    x_copy(1, 1).start()
    x_copy(0, 0).wait()
    _, token_only, _ = phases(0, 0, 0)
    token_only()

    def step(i, carry):
        slot = lax.rem(i, 2)

        @pl.when(i + 1 < n_tiles)
        def _():
            x_copy(i + 1, 1 - slot).start()

        x_copy(i, slot).wait()

        @pl.when(i >= 3)
        def _():
            o_copy(i - 3, slot).wait()

        pair, _, _ = phases(slot, i, i - 1)
        pair()
        o_copy(i - 1, slot).start()
        return carry

    lax.fori_loop(1, n_tiles, step, 0)
    last_slot = n_tiles % 2
    o_copy(n_tiles - 3, last_slot).wait()
    _, _, channel_only = phases(last_slot, n_tiles - 1, n_tiles - 1)
    channel_only()
    o_copy(n_tiles - 1, last_slot).start()
    o_copy(n_tiles - 2, 1 - last_slot).wait()
    o_copy(n_tiles - 1, last_slot).wait()


def _whole(shape):
    return pl.BlockSpec(shape, lambda i: (0,) * len(shape))


@jax.jit
def kernel(x, norm_mix, w_in, pool_w, pool_scale, w_pool_proj, conv_w, w_conv_out, w_o,
           norm_ffn, w_up, ffn_conv_w, ffn_conv_b, w_down, norm_final):
    batch, seq, d_model = x.shape
    ts = SEQ_TILE
    assert d_model == D_MODEL and seq % ts == 0 and D_FF % FFN_CHUNK == 0
    assert POOL_GROUP_DIM == MXU_TILE and N_FFN_CHUNKS == 11
    assert norm_mix.shape[0] == 1, "single-layer block"
    bf16 = jnp.bfloat16
    tiles_per_row = seq // ts
    n_tiles = batch * tiles_per_row
    assert n_tiles >= 2

    matmul_weights = {2, 3, 5, 7, 8, 10, 13}
    operands = (
        x,
        norm_mix[0][None, :],
        w_in[0],
        pool_w[0].reshape(N_POOL_GROUPS * POOL_GROUP_DIM, POOL_GROUP_DIM),
        pool_scale[0][None, :],
        w_pool_proj[0],
        conv_w[0],
        w_conv_out[0],
        w_o[0],
        norm_ffn[0][None, :],
        w_up[0],
        ffn_conv_w[0],
        ffn_conv_b[0][None, :],
        w_down[0],
        norm_final[None, :],
    )

    any_spec = pl.BlockSpec(memory_space=pl.ANY)
    in_specs = [any_spec] + [
        any_spec if k in matmul_weights else _whole(op.shape)
        for k, op in enumerate(operands) if k > 0]
    resident_bf16 = [pltpu.VMEM(operands[k].shape, bf16) for k in (2, 5, 7, 8, 10, 13)]

    return pl.pallas_call(
        functools.partial(_block_kernel, tiles_per_row, n_tiles),
        grid=(1,),
        in_specs=in_specs,
        out_specs=any_spec,
        out_shape=jax.ShapeDtypeStruct(x.shape, x.dtype),
        scratch_shapes=[
            pltpu.VMEM((2, ts, D_MODEL), jnp.float32),
            pltpu.VMEM((2, ts, D_MODEL), jnp.float32),
            pltpu.SemaphoreType.DMA((2, 2)),
            pltpu.VMEM((D_MODEL // LANES, POOL_HIST + ts, LANES), jnp.float32),
            pltpu.VMEM((D_MODEL // LANES, CONV_HIST + ts, LANES), jnp.float32),
            pltpu.VMEM((ts, D_MODEL), jnp.float32),
            pltpu.VMEM((ts, D_MODEL), bf16),
        ] + resident_bf16 + [
            pltpu.VMEM((2 * FFN_CHUNK // LANES, CONV_HIST + ts, LANES), jnp.float32)
            for _ in range(N_FFN_CHUNKS)
        ],
        compiler_params=pltpu.CompilerParams(
            dimension_semantics=("arbitrary",),
            vmem_limit_bytes=V7X_VMEM_LIMIT_BYTES,
        ),
        name="hybrid_block",
    )(*operands)
```

```python
import functools

import jax
import jax.numpy as jnp
from jax import lax
from jax.experimental import pallas as pl
from jax.experimental.pallas import tpu as pltpu

D_MODEL = 1024
N_POOL_GROUPS = 4
POOL_WINDOWS = (2, 4, 8, 16)
POOL_GROUP_DIM = D_MODEL // N_POOL_GROUPS
MXU_TILE = 256
D_FF = 2816
KSIZE = 3
RMS_EPS = 1e-6
LANES = 128
BF16_ROWS = 16

SEQ_TILE = 256
POOL_HIST = 16
CONV_HIST = 8
FFN_CHUNK = 256
N_FFN_CHUNKS = D_FF // FFN_CHUNK
UP_AHEAD = 2
CAST_CHUNK_BYTES = 1024 * 1024
CAST_SLOTS = 6
V7X_VMEM_LIMIT_BYTES = 60 * 1024 * 1024


def _rms_norm(x, g):
    inv = lax.rsqrt(jnp.mean(x * x, axis=-1, keepdims=True) + RMS_EPS)
    return x * inv * g


def _dot(a, b):
    return jnp.dot(a, b, preferred_element_type=jnp.float32)


def _slab_store(buf, row0, val):
    for j in range(buf.shape[0]):
        buf[j, row0:row0 + val.shape[0], :] = val[:, j * LANES:(j + 1) * LANES]


def _slab_rows(buf, j, start, rows):
    return buf[pl.ds(j, 1, stride=2), pl.ds(start, rows), :][0]


def _slab_window(buf, start, rows):
    return jnp.concatenate([_slab_rows(buf, j, start, rows) for j in range(buf.shape[0])], axis=-1)


def _carry_history(buf, hist, rows, keep):
    buf[:, 0:hist, :] = jnp.where(keep, buf[:, rows:rows + hist, :], 0.0)


def _causal_taps(buf, hist, rows, w):
    out = None
    for k in range(KSIZE):
        term = w[k:k + 1, :] * _slab_window(buf, hist - (KSIZE - 1) + k, rows)
        out = term if out is None else out + term
    return out


def _block(j):
    return slice(j * MXU_TILE, (j + 1) * MXU_TILE)


def _tile_pair(x_ref, x_next_ref, h_ref, h_next_ref, o_ref, x1_buf, h2_buf, t_tok, t_ch,
               tiles_per_row,
               g_mix_ref, w_in_ref, w_pool_ref, conv_w_ref,
               w_conv_out_ref, w_o_ref, g_ffn_ref, w_up_ref, ffn_conv_w_ref, ffn_conv_b_ref,
               w_down_ref, g_final_ref, zp_buf, cv_buf, u_bufs):
    ts = x_ref.shape[0]
    bf16 = jnp.bfloat16
    tok_continues = t_tok + 1 < tiles_per_row
    ch_continues = t_ch + 1 < tiles_per_row

    def gate_val(ref, c):
        gate = ref[:, c * FFN_CHUNK:(c + 1) * FFN_CHUNK]
        val = ref[:, D_FF + c * FFN_CHUNK:D_FF + (c + 1) * FFN_CHUNK]
        return jnp.concatenate([gate, val], axis=1)

    def up_proj(c):
        _slab_store(u_bufs[c], CONV_HIST, _dot(h2_buf[...], gate_val(w_up_ref, c)))

    def ffn_act(c):
        u_buf = u_bufs[c]
        u = (_causal_taps(u_buf, CONV_HIST, ts, gate_val(ffn_conv_w_ref, c))
             + gate_val(ffn_conv_b_ref, c))
        _carry_history(u_buf, CONV_HIST, ts, ch_continues)
        return (jax.nn.silu(u[:, :FFN_CHUNK]) * u[:, FFN_CHUNK:]).astype(bf16)

    def down_proj(c, act):
        return _dot(act, w_down_ref[c * FFN_CHUNK:(c + 1) * FFN_CHUNK, :])

    def in_proj(seg):
        return _dot(h_ref[...], w_in_ref[:, seg * D_MODEL:(seg + 1) * D_MODEL])

    def blocked_dot(lhs_block, w_ref, blocks):
        out = None
        for j in blocks:
            part = _dot(lhs_block(j), w_ref[_block(j), :])
            out = part if out is None else out + part
        return out

    all_blocks = range(D_MODEL // MXU_TILE)

    def pool_windows():
        pos = t_tok * ts + lax.broadcasted_iota(jnp.int32, (ts, 1), 0) + 1
        slabs_per_group = POOL_GROUP_DIM // LANES
        pooled = []
        for gi, win in enumerate(POOL_WINDOWS):
            def window(back):
                return jnp.concatenate(
                    [_slab_rows(zp_buf, gi * slabs_per_group + j, POOL_HIST - back, ts)
                     for j in range(slabs_per_group)], axis=-1)
            ug = window(0)
            s = ug
            for k in range(1, win):
                s = s + window(k)
            inv_count = 1.0 / jnp.minimum(pos, win).astype(jnp.float32)
            pooled.append((s * inv_count - ug).astype(bf16))
        _carry_history(zp_buf, POOL_HIST, ts, tok_continues)
        return pooled

    v = {}

    def t_conv_c():
        v["z_c"] = in_proj(2)

    def t_windows():
        v["pooled"] = pool_windows()

    def t_conv_v():
        _slab_store(cv_buf, CONV_HIST, v["z_c"] * in_proj(3))

    def t_conv_taps():
        v["conv"] = _causal_taps(cv_buf, CONV_HIST, ts, conv_w_ref[...])
        _carry_history(cv_buf, CONV_HIST, ts, tok_continues)

    def t_conv_b():
        v["conv_in"] = (in_proj(1) * v["conv"]).astype(bf16)

    def t_pool_proj():
        v["y_pool"] = blocked_dot(lambda j: v["pooled"][j], w_pool_ref, all_blocks)

    def t_gate_pool():
        v["gated_pool"] = jax.nn.sigmoid(in_proj(4)) * v["y_pool"]

    def t_gate_conv():
        v["gate_conv"] = jax.nn.sigmoid(in_proj(5))

    def t_conv_out():
        y_conv = blocked_dot(lambda j: v["conv_in"][:, _block(j)], w_conv_out_ref, all_blocks)
        v["merged"] = (v["gated_pool"] + v["gate_conv"] * y_conv).astype(bf16)

    def t_out_proj_a():
        v["x1"] = v["x"] + blocked_dot(lambda j: v["merged"][:, _block(j)], w_o_ref, (0, 1))

    def t_out_proj_b():
        x1_buf[...] = v["x1"] + blocked_dot(lambda j: v["merged"][:, _block(j)], w_o_ref, (2, 3))

    def t_next_norm():
        h_next_ref[...] = _rms_norm(x_next_ref[...], g_mix_ref[...]).astype(bf16)

    def token_head():
        v["x"] = x_ref[...]
        _slab_store(zp_buf, POOL_HIST, in_proj(0))

    def up_head():
        h2_buf[...] = _rms_norm(x1_buf[...], g_ffn_ref[...]).astype(bf16)
        for c in range(UP_AHEAD):
            up_proj(c)

    def main():
        token_pieces = (t_conv_c, t_windows, t_conv_v, t_conv_taps, t_next_norm, t_conv_b,
                        t_pool_proj, t_gate_pool, t_gate_conv, t_conv_out, t_out_proj_a)
        x2 = x1_buf[...]
        for c in range(N_FFN_CHUNKS):
            act = ffn_act(c)
            if c + UP_AHEAD < N_FFN_CHUNKS:
                up_proj(c + UP_AHEAD)
            token_pieces[c]()
            x2 = x2 + down_proj(c, act)
        v["x2"] = x2

    def tail():
        o_ref[...] = _rms_norm(v["x2"], g_final_ref[...])
        t_out_proj_b()

    def pair():
        token_head()
        up_head()
        main()
        tail()

    def token_only():
        token_head()
        for piece in (t_conv_c, t_windows, t_conv_v, t_conv_taps, t_next_norm, t_conv_b,
                      t_pool_proj, t_gate_pool, t_gate_conv, t_conv_out, t_out_proj_a,
                      t_out_proj_b):
            piece()

    def channel_only():
        up_head()
        x2 = x1_buf[...]
        for c in range(N_FFN_CHUNKS):
            act = ffn_act(c)
            if c + UP_AHEAD < N_FFN_CHUNKS:
                up_proj(c + UP_AHEAD)
            x2 = x2 + down_proj(c, act)
        o_ref[...] = _rms_norm(x2, g_final_ref[...])

    return pair, token_only, channel_only


def _cast_chunk_rows(rows, cols):
    budget = max(BF16_ROWS, CAST_CHUNK_BYTES // (4 * cols))
    return max(r for r in range(BF16_ROWS, min(rows, budget) + 1, BF16_ROWS) if rows % r == 0)


def _load_as_bf16(src_hbm, dst, chunk_rows):
    rows, cols = src_hbm.shape
    assert rows % chunk_rows == 0 and dst.shape == src_hbm.shape
    n_chunks = rows // chunk_rows
    ahead = CAST_SLOTS - 1

    def scoped(stage, sems):
        def chunk_copy(i, slot):
            return pltpu.make_async_copy(
                src_hbm.at[pl.ds(i * chunk_rows, chunk_rows), :], stage.at[slot], sems.at[slot])

        for k in range(min(ahead, n_chunks)):
            chunk_copy(k, k).start()

        def body(i, carry):
            @pl.when(i + ahead < n_chunks)
            def _():
                chunk_copy(i + ahead, lax.rem(i + ahead, CAST_SLOTS)).start()

            slot = lax.rem(i, CAST_SLOTS)
            chunk_copy(i, slot).wait()
            row0 = pl.multiple_of(i * chunk_rows, chunk_rows)
            dst[pl.ds(row0, chunk_rows), :] = stage[slot].astype(jnp.bfloat16)
            return carry

        lax.fori_loop(0, n_chunks, body, 0)

    pl.run_scoped(scoped, pltpu.VMEM((CAST_SLOTS, chunk_rows, cols), jnp.float32),
                  pltpu.SemaphoreType.DMA((CAST_SLOTS,)))


def _fold_pool_weights(pool_w_hbm, pool_scale_ref, w_pool_proj_hbm, dst):
    def scoped(pool_w, w_pool_proj, sems):
        copies = (pltpu.make_async_copy(pool_w_hbm, pool_w, sems.at[0]),
                  pltpu.make_async_copy(w_pool_proj_hbm, w_pool_proj, sems.at[1]))
        for copy in copies:
            copy.start()
        for copy in copies:
            copy.wait()
        for gi in range(N_POOL_GROUPS):
            scaled = pool_w[_block(gi), :] * pool_scale_ref[:, _block(gi)]
            dst[_block(gi), :] = jnp.dot(
                scaled, w_pool_proj[_block(gi), :], precision=lax.Precision.HIGHEST,
                preferred_element_type=jnp.float32).astype(jnp.bfloat16)

    pl.run_scoped(scoped, pltpu.VMEM(pool_w_hbm.shape, jnp.float32),
                  pltpu.VMEM(w_pool_proj_hbm.shape, jnp.float32), pltpu.SemaphoreType.DMA((2,)))


def _block_kernel(tiles_per_row, n_tiles, x_hbm, *refs):
    (g_mix_ref, w_in_hbm, pool_w_hbm, pool_scale_ref, w_pool_proj_hbm, conv_w_ref,
     w_conv_out_hbm, w_o_hbm, g_ffn_ref, w_up_hbm, ffn_conv_w_ref, ffn_conv_b_ref, w_down_hbm,
     g_final_ref, o_hbm, x_stage, o_stage, x_sems, o_sems, zp_buf, cv_buf, x1_buf, h2_buf,
     h_buf, w_in_ref, w_pool_ref, w_conv_out_ref, w_o_ref, w_up_ref, w_down_ref) = refs[:30]
    u_bufs = refs[30:]
    weights = (g_mix_ref, w_in_ref, w_pool_ref, conv_w_ref,
               w_conv_out_ref, w_o_ref, g_ffn_ref, w_up_ref, ffn_conv_w_ref, ffn_conv_b_ref,
               w_down_ref, g_final_ref)
    ts = x_stage.shape[1]

    def tile_of(ref, tile):
        row0 = pl.multiple_of(lax.rem(tile, tiles_per_row) * ts, ts)
        return ref.at[tile // tiles_per_row, pl.ds(row0, ts), :]

    def x_copy(tile, slot):
        return pltpu.make_async_copy(tile_of(x_hbm, tile), x_stage.at[slot], x_sems.at[slot])

    def o_copy(tile, slot):
        return pltpu.make_async_copy(o_stage.at[slot], tile_of(o_hbm, tile), o_sems.at[slot])

    x_copy(0, 0).start()
    zp_buf[:, 0:POOL_HIST, :] = jnp.zeros((zp_buf.shape[0], POOL_HIST, LANES), jnp.float32)
    for buf in (cv_buf,) + tuple(u_bufs):
        buf[:, 0:CONV_HIST, :] = jnp.zeros((buf.shape[0], CONV_HIST, LANES), jnp.float32)
    for src, dst in ((w_in_hbm, w_in_ref), (w_conv_out_hbm, w_conv_out_ref), (w_o_hbm, w_o_ref),
                     (w_up_hbm, w_up_ref), (w_down_hbm, w_down_ref)):
        _load_as_bf16(src, dst, _cast_chunk_rows(*src.shape))
    _fold_pool_weights(pool_w_hbm, pool_scale_ref, w_pool_proj_hbm, w_pool_ref)

    def phases(i, tok_tile, ch_tile):
        def mod(a, m):
            return a % m if isinstance(a, int) else lax.rem(a, m)

        return _tile_pair(
            x_stage.at[mod(i, 3)], x_stage.at[mod(i + 1, 3)], h_buf.at[mod(i, 2)],
            h_buf.at[mod(i + 1, 2)], o_stage.at[mod(i, 2)], x1_buf, h2_buf,
            lax.rem(tok_tile, tiles_per_row), lax.rem(ch_tile, tiles_per_row), tiles_per_row,
            *weights, zp_buf, cv_buf, u_bufs)

    x_copy(1, 1).start()
    x_copy(2, 2).start()
    x_copy(0, 0).wait()
    h_buf[0] = _rms_norm(x_stage[0], g_mix_ref[...]).astype(jnp.bfloat16)
    x_copy(1, 1).wait()
    _, token_only, _ = phases(0, 0, 0)
    token_only()

    def step(i, carry):
        @pl.when(i + 2 < n_tiles)
        def _():
            x_copy(i + 2, lax.rem(i + 2, 3)).start()

        @pl.when(i + 1 < n_tiles)
        def _():
            x_copy(i + 1, lax.rem(i + 1, 3)).wait()

        slot = lax.rem(i, 2)

        @pl.when(i >= 3)
        def _():
            o_copy(i - 3, slot).wait()

        pair, _, _ = phases(i, i, i - 1)
        pair()
        o_copy(i - 1, slot).start()
        return carry

    lax.fori_loop(1, n_tiles, step, 0)
    last_slot = n_tiles % 2
    o_copy(n_tiles - 3, last_slot).wait()
    _, _, channel_only = phases(n_tiles, n_tiles - 1, n_tiles - 1)
    channel_only()
    o_copy(n_tiles - 1, last_slot).start()
    o_copy(n_tiles - 2, 1 - last_slot).wait()
    o_copy(n_tiles - 1, last_slot).wait()


def _whole(shape):
    return pl.BlockSpec(shape, lambda i: (0,) * len(shape))


@jax.jit
def kernel(x, norm_mix, w_in, pool_w, pool_scale, w_pool_proj, conv_w, w_conv_out, w_o,
           norm_ffn, w_up, ffn_conv_w, ffn_conv_b, w_down, norm_final):
    batch, seq, d_model = x.shape
    ts = SEQ_TILE
    assert d_model == D_MODEL and seq % ts == 0 and D_FF % FFN_CHUNK == 0
    assert POOL_GROUP_DIM == MXU_TILE and N_FFN_CHUNKS == 11
    assert norm_mix.shape[0] == 1, "single-layer block"
    bf16 = jnp.bfloat16
    tiles_per_row = seq // ts
    n_tiles = batch * tiles_per_row
    assert n_tiles >= 2

    matmul_weights = {2, 3, 5, 7, 8, 10, 13}
    operands = (
        x,
        norm_mix[0][None, :],
        w_in[0],
        pool_w[0].reshape(N_POOL_GROUPS * POOL_GROUP_DIM, POOL_GROUP_DIM),
        pool_scale[0][None, :],
        w_pool_proj[0],
        conv_w[0],
        w_conv_out[0],
        w_o[0],
        norm_ffn[0][None, :],
        w_up[0],
        ffn_conv_w[0],
        ffn_conv_b[0][None, :],
        w_down[0],
        norm_final[None, :],
    )

    any_spec = pl.BlockSpec(memory_space=pl.ANY)
    in_specs = [any_spec] + [
        any_spec if k in matmul_weights else _whole(op.shape)
        for k, op in enumerate(operands) if k > 0]
    resident_bf16 = [pltpu.VMEM(operands[k].shape, bf16) for k in (2, 5, 7, 8, 10, 13)]

    return pl.pallas_call(
        functools.partial(_block_kernel, tiles_per_row, n_tiles),
        grid=(1,),
        in_specs=in_specs,
        out_specs=any_spec,
        out_shape=jax.ShapeDtypeStruct(x.shape, x.dtype),
        scratch_shapes=[
            pltpu.VMEM((3, ts, D_MODEL), jnp.float32),
            pltpu.VMEM((2, ts, D_MODEL), jnp.float32),
            pltpu.SemaphoreType.DMA((3,)),
            pltpu.SemaphoreType.DMA((2,)),
            pltpu.VMEM((D_MODEL // LANES, POOL_HIST + ts, LANES), jnp.float32),
            pltpu.VMEM((D_MODEL // LANES, CONV_HIST + ts, LANES), jnp.float32),
            pltpu.VMEM((ts, D_MODEL), jnp.float32),
            pltpu.VMEM((ts, D_MODEL), bf16),
            pltpu.VMEM((2, ts, D_MODEL), bf16),
        ] + resident_bf16 + [
            pltpu.VMEM((2 * FFN_CHUNK // LANES, CONV_HIST + ts, LANES), jnp.float32)
            for _ in range(N_FFN_CHUNKS)
        ],
        compiler_params=pltpu.CompilerParams(
            dimension_semantics=("arbitrary",),
            vmem_limit_bytes=V7X_VMEM_LIMIT_BYTES,
        ),
        name="hybrid_block",
    )(*operands)
```

```python
import functools

import jax
import jax.numpy as jnp
from jax import lax
from jax.experimental import pallas as pl
from jax.experimental.pallas import tpu as pltpu

D_MODEL = 1024
N_POOL_GROUPS = 4
POOL_WINDOWS = (2, 4, 8, 16)
POOL_GROUP_DIM = D_MODEL // N_POOL_GROUPS
MXU_TILE = 256
D_FF = 2816
KSIZE = 3
RMS_EPS = 1e-6
LANES = 128
BF16_ROWS = 16

SEQ_TILE = 256
POOL_HIST = 16
CONV_HIST = 8
FFN_CHUNK = 256
N_FFN_CHUNKS = D_FF // FFN_CHUNK
UP_AHEAD = 2
CAST_CHUNK_BYTES = 1024 * 1024
CAST_SLOTS = 6
V7X_VMEM_LIMIT_BYTES = 60 * 1024 * 1024


def _rms_norm(x, g):
    inv = lax.rsqrt(jnp.mean(x * x, axis=-1, keepdims=True) + RMS_EPS)
    return x * inv * g


def _dot(a, b):
    return jnp.dot(a, b, preferred_element_type=jnp.float32)


def _slab_store(buf, row0, val):
    for j in range(buf.shape[0]):
        buf[j, row0:row0 + val.shape[0], :] = val[:, j * LANES:(j + 1) * LANES]


def _slab_rows(buf, j, start, rows):
    return buf[pl.ds(j, 1, stride=2), pl.ds(start, rows), :][0]


def _slab_window(buf, start, rows):
    return jnp.concatenate([_slab_rows(buf, j, start, rows) for j in range(buf.shape[0])], axis=-1)


def _carry_history(buf, hist, rows, keep):
    buf[:, 0:hist, :] = jnp.where(keep, buf[:, rows:rows + hist, :], 0.0)


def _causal_taps(buf, hist, rows, w):
    out = None
    for k in range(KSIZE):
        term = w[k:k + 1, :] * _slab_window(buf, hist - (KSIZE - 1) + k, rows)
        out = term if out is None else out + term
    return out


def _block(j):
    return slice(j * MXU_TILE, (j + 1) * MXU_TILE)


def _tile_pair(x_ref, o_ref, x1_buf, h2_buf, t_tok, t_ch, tiles_per_row,
               g_mix_ref, w_in_ref, w_pool_ref, conv_w_ref,
               w_conv_out_ref, w_o_ref, g_ffn_ref, w_up_ref, ffn_conv_w_ref, ffn_conv_b_ref,
               w_down_ref, g_final_ref, zp_buf, cv_buf, u_bufs):
    ts = x_ref.shape[0]
    bf16 = jnp.bfloat16
    tok_continues = t_tok + 1 < tiles_per_row
    ch_continues = t_ch + 1 < tiles_per_row

    def gate_val(ref, c):
        gate = ref[:, c * FFN_CHUNK:(c + 1) * FFN_CHUNK]
        val = ref[:, D_FF + c * FFN_CHUNK:D_FF + (c + 1) * FFN_CHUNK]
        return jnp.concatenate([gate, val], axis=1)

    def up_proj(c):
        _slab_store(u_bufs[c], CONV_HIST, _dot(h2_buf[...], gate_val(w_up_ref, c)))

    def ffn_act(c):
        u_buf = u_bufs[c]
        u = (_causal_taps(u_buf, CONV_HIST, ts, gate_val(ffn_conv_w_ref, c))
             + gate_val(ffn_conv_b_ref, c))
        _carry_history(u_buf, CONV_HIST, ts, ch_continues)
        return (jax.nn.silu(u[:, :FFN_CHUNK]) * u[:, FFN_CHUNK:]).astype(bf16)

    def down_proj(c, act):
        return _dot(act, w_down_ref[c * FFN_CHUNK:(c + 1) * FFN_CHUNK, :])

    def in_proj(seg):
        return _dot(v["h"], w_in_ref[:, seg * D_MODEL:(seg + 1) * D_MODEL])

    def blocked_dot(lhs_block, w_ref, blocks):
        out = None
        for j in blocks:
            part = _dot(lhs_block(j), w_ref[_block(j), :])
            out = part if out is None else out + part
        return out

    all_blocks = range(D_MODEL // MXU_TILE)

    def pool_windows():
        pos = t_tok * ts + lax.broadcasted_iota(jnp.int32, (ts, 1), 0) + 1
        slabs_per_group = POOL_GROUP_DIM // LANES
        pooled = []
        for gi, win in enumerate(POOL_WINDOWS):
            def window(back):
                return jnp.concatenate(
                    [_slab_rows(zp_buf, gi * slabs_per_group + j, POOL_HIST - back, ts)
                     for j in range(slabs_per_group)], axis=-1)
            ug = window(0)
            s = ug
            for k in range(1, win):
                s = s + window(k)
            inv_count = 1.0 / jnp.minimum(pos, win).astype(jnp.float32)
            pooled.append((s * inv_count - ug).astype(bf16))
        _carry_history(zp_buf, POOL_HIST, ts, tok_continues)
        return pooled

    v = {}

    def t_conv_c():
        v["z_c"] = in_proj(2)

    def t_windows():
        v["pooled"] = pool_windows()

    def t_conv_v():
        _slab_store(cv_buf, CONV_HIST, v["z_c"] * in_proj(3))

    def t_conv_taps():
        v["conv"] = _causal_taps(cv_buf, CONV_HIST, ts, conv_w_ref[...])
        _carry_history(cv_buf, CONV_HIST, ts, tok_continues)

    def t_conv_b():
        v["conv_in"] = (in_proj(1) * v["conv"]).astype(bf16)

    def t_pool_proj():
        v["y_pool"] = blocked_dot(lambda j: v["pooled"][j], w_pool_ref, all_blocks)

    def t_gate_pool():
        v["gated_pool"] = jax.nn.sigmoid(in_proj(4)) * v["y_pool"]

    def t_gate_conv():
        v["gate_conv"] = jax.nn.sigmoid(in_proj(5))

    def t_conv_out():
        y_conv = blocked_dot(lambda j: v["conv_in"][:, _block(j)], w_conv_out_ref, all_blocks)
        v["merged"] = (v["gated_pool"] + v["gate_conv"] * y_conv).astype(bf16)

    def t_out_proj_a():
        v["x1"] = v["x"] + blocked_dot(lambda j: v["merged"][:, _block(j)], w_o_ref, (0, 1))

    def t_out_proj_b():
        x1 = v["x1"] + blocked_dot(lambda j: v["merged"][:, _block(j)], w_o_ref, (2, 3))
        x1_buf[...] = x1
        h2_buf[...] = _rms_norm(x1, g_ffn_ref[...]).astype(bf16)

    def up_head():
        for c in range(UP_AHEAD):
            up_proj(c)

    def token_head():
        v["x"] = x_ref[...]
        v["h"] = _rms_norm(v["x"], g_mix_ref[...]).astype(bf16)
        _slab_store(zp_buf, POOL_HIST, in_proj(0))

    def main():
        token_pieces = (t_conv_c, t_windows, t_conv_v, t_conv_taps, None, t_conv_b,
                        t_pool_proj, t_gate_pool, t_gate_conv, t_conv_out, t_out_proj_a)
        x2 = x1_buf[...]
        for c in range(N_FFN_CHUNKS):
            act = ffn_act(c)
            if c + UP_AHEAD < N_FFN_CHUNKS:
                up_proj(c + UP_AHEAD)
            if token_pieces[c] is not None:
                token_pieces[c]()
            x2 = x2 + down_proj(c, act)
        v["x2"] = x2

    def tail():
        o_ref[...] = _rms_norm(v["x2"], g_final_ref[...])
        t_out_proj_b()

    def pair():
        up_head()
        token_head()
        main()
        tail()

    def token_only():
        token_head()
        for piece in (t_conv_c, t_windows, t_conv_v, t_conv_taps, t_conv_b, t_pool_proj,
                      t_gate_pool, t_gate_conv, t_conv_out, t_out_proj_a, t_out_proj_b):
            piece()

    def channel_only():
        up_head()
        x2 = x1_buf[...]
        for c in range(N_FFN_CHUNKS):
            act = ffn_act(c)
            if c + UP_AHEAD < N_FFN_CHUNKS:
                up_proj(c + UP_AHEAD)
            x2 = x2 + down_proj(c, act)
        o_ref[...] = _rms_norm(x2, g_final_ref[...])

    return pair, token_only, channel_only


def _cast_chunk_rows(rows, cols):
    budget = max(BF16_ROWS, CAST_CHUNK_BYTES // (4 * cols))
    return max(r for r in range(BF16_ROWS, min(rows, budget) + 1, BF16_ROWS) if rows % r == 0)


def _load_as_bf16(src_hbm, dst, chunk_rows):
    rows, cols = src_hbm.shape
    assert rows % chunk_rows == 0 and dst.shape == src_hbm.shape
    n_chunks = rows // chunk_rows
    ahead = CAST_SLOTS - 1

    def scoped(stage, sems):
        def chunk_copy(i, slot):
            return pltpu.make_async_copy(
                src_hbm.at[pl.ds(i * chunk_rows, chunk_rows), :], stage.at[slot], sems.at[slot])

        for k in range(min(ahead, n_chunks)):
            chunk_copy(k, k).start()

        def body(i, carry):
            @pl.when(i + ahead < n_chunks)
            def _():
                chunk_copy(i + ahead, lax.rem(i + ahead, CAST_SLOTS)).start()

            slot = lax.rem(i, CAST_SLOTS)
            chunk_copy(i, slot).wait()
            row0 = pl.multiple_of(i * chunk_rows, chunk_rows)
            dst[pl.ds(row0, chunk_rows), :] = stage[slot].astype(jnp.bfloat16)
            return carry

        lax.fori_loop(0, n_chunks, body, 0)

    pl.run_scoped(scoped, pltpu.VMEM((CAST_SLOTS, chunk_rows, cols), jnp.float32),
                  pltpu.SemaphoreType.DMA((CAST_SLOTS,)))


def _fold_pool_weights(pool_w_hbm, pool_scale_ref, w_pool_proj_hbm, dst):
    def scoped(pool_w, w_pool_proj, sems):
        copies = (pltpu.make_async_copy(pool_w_hbm, pool_w, sems.at[0]),
                  pltpu.make_async_copy(w_pool_proj_hbm, w_pool_proj, sems.at[1]))
        for copy in copies:
            copy.start()
        for copy in copies:
            copy.wait()
        for gi in range(N_POOL_GROUPS):
            scaled = pool_w[_block(gi), :] * pool_scale_ref[:, _block(gi)]
            dst[_block(gi), :] = jnp.dot(
                scaled, w_pool_proj[_block(gi), :], precision=lax.Precision.HIGHEST,
                preferred_element_type=jnp.float32).astype(jnp.bfloat16)

    pl.run_scoped(scoped, pltpu.VMEM(pool_w_hbm.shape, jnp.float32),
                  pltpu.VMEM(w_pool_proj_hbm.shape, jnp.float32), pltpu.SemaphoreType.DMA((2,)))


def _block_kernel(tiles_per_row, n_tiles, x_hbm, *refs):
    (g_mix_ref, w_in_hbm, pool_w_hbm, pool_scale_ref, w_pool_proj_hbm, conv_w_ref,
     w_conv_out_hbm, w_o_hbm, g_ffn_ref, w_up_hbm, ffn_conv_w_ref, ffn_conv_b_ref, w_down_hbm,
     g_final_ref, o_hbm, x_stage, o_stage, io_sems, zp_buf, cv_buf, x1_buf, h2_buf,
     w_in_ref, w_pool_ref, w_conv_out_ref, w_o_ref, w_up_ref, w_down_ref) = refs[:28]
    u_bufs = refs[28:]
    weights = (g_mix_ref, w_in_ref, w_pool_ref, conv_w_ref,
               w_conv_out_ref, w_o_ref, g_ffn_ref, w_up_ref, ffn_conv_w_ref, ffn_conv_b_ref,
               w_down_ref, g_final_ref)
    ts = x_stage.shape[1]

    def tile_of(ref, tile):
        row0 = pl.multiple_of(lax.rem(tile, tiles_per_row) * ts, ts)
        return ref.at[tile // tiles_per_row, pl.ds(row0, ts), :]

    def x_copy(tile, slot):
        return pltpu.make_async_copy(tile_of(x_hbm, tile), x_stage.at[slot], io_sems.at[0, slot])

    def o_copy(tile, slot):
        return pltpu.make_async_copy(o_stage.at[slot], tile_of(o_hbm, tile), io_sems.at[1, slot])

    x_copy(0, 0).start()
    zp_buf[:, 0:POOL_HIST, :] = jnp.zeros((zp_buf.shape[0], POOL_HIST, LANES), jnp.float32)
    for buf in (cv_buf,) + tuple(u_bufs):
        buf[:, 0:CONV_HIST, :] = jnp.zeros((buf.shape[0], CONV_HIST, LANES), jnp.float32)
    for src, dst in ((w_in_hbm, w_in_ref), (w_conv_out_hbm, w_conv_out_ref), (w_o_hbm, w_o_ref),
                     (w_up_hbm, w_up_ref), (w_down_hbm, w_down_ref)):
        _load_as_bf16(src, dst, _cast_chunk_rows(*src.shape))
    _fold_pool_weights(pool_w_hbm, pool_scale_ref, w_pool_proj_hbm, w_pool_ref)

    def phases(slot, tok_tile, ch_tile):
        return _tile_pair(
            x_stage.at[slot], o_stage.at[slot], x1_buf, h2_buf,
            lax.rem(tok_tile, tiles_per_row), lax.rem(ch_tile, tiles_per_row), tiles_per_row,
            *weights, zp_buf, cv_buf, u_bufs)

    x_copy(1, 1).start()
    x_copy(0, 0).wait()
    _, token_only, _ = phases(0, 0, 0)
    token_only()

    def step(i, carry):
        slot = lax.rem(i, 2)

        @pl.when(i + 1 < n_tiles)
        def _():
            x_copy(i + 1, 1 - slot).start()

        x_copy(i, slot).wait()

        @pl.when(i >= 3)
        def _():
            o_copy(i - 3, slot).wait()

        pair, _, _ = phases(slot, i, i - 1)
        pair()
        o_copy(i - 1, slot).start()
        return carry

    lax.fori_loop(1, n_tiles, step, 0)
    last_slot = n_tiles % 2
    o_copy(n_tiles - 3, last_slot).wait()
    _, _, channel_only = phases(last_slot, n_tiles - 1, n_tiles - 1)
    channel_only()
    o_copy(n_tiles - 1, last_slot).start()
    o_copy(n_tiles - 2, 1 - last_slot).wait()
    o_copy(n_tiles - 1, last_slot).wait()


def _whole(shape):
    return pl.BlockSpec(shape, lambda i: (0,) * len(shape))


@jax.jit
def kernel(x, norm_mix, w_in, pool_w, pool_scale, w_pool_proj, conv_w, w_conv_out, w_o,
           norm_ffn, w_up, ffn_conv_w, ffn_conv_b, w_down, norm_final):
    batch, seq, d_model = x.shape
    ts = SEQ_TILE
    assert d_model == D_MODEL and seq % ts == 0 and D_FF % FFN_CHUNK == 0
    assert POOL_GROUP_DIM == MXU_TILE and N_FFN_CHUNKS == 11
    assert norm_mix.shape[0] == 1, "single-layer block"
    bf16 = jnp.bfloat16
    tiles_per_row = seq // ts
    n_tiles = batch * tiles_per_row
    assert n_tiles >= 3, "the output-copy bookkeeping assumes at least three tiles"

    matmul_weights = {2, 3, 5, 7, 8, 10, 13}
    operands = (
        x,
        norm_mix[0][None, :],
        w_in[0],
        pool_w[0].reshape(N_POOL_GROUPS * POOL_GROUP_DIM, POOL_GROUP_DIM),
        pool_scale[0][None, :],
        w_pool_proj[0],
        conv_w[0],
        w_conv_out[0],
        w_o[0],
        norm_ffn[0][None, :],
        w_up[0],
        ffn_conv_w[0],
        ffn_conv_b[0][None, :],
        w_down[0],
        norm_final[None, :],
    )

    any_spec = pl.BlockSpec(memory_space=pl.ANY)
    in_specs = [any_spec] + [
        any_spec if k in matmul_weights else _whole(op.shape)
        for k, op in enumerate(operands) if k > 0]
    resident_bf16 = [pltpu.VMEM(operands[k].shape, bf16) for k in (2, 5, 7, 8, 10, 13)]

    return pl.pallas_call(
        functools.partial(_block_kernel, tiles_per_row, n_tiles),
        grid=(1,),
        in_specs=in_specs,
        out_specs=any_spec,
        out_shape=jax.ShapeDtypeStruct(x.shape, x.dtype),
        scratch_shapes=[
            pltpu.VMEM((2, ts, D_MODEL), jnp.float32),
            pltpu.VMEM((2, ts, D_MODEL), jnp.float32),
            pltpu.SemaphoreType.DMA((2, 2)),
            pltpu.VMEM((D_MODEL // LANES, POOL_HIST + ts, LANES), jnp.float32),
            pltpu.VMEM((D_MODEL // LANES, CONV_HIST + ts, LANES), jnp.float32),
            pltpu.VMEM((ts, D_MODEL), jnp.float32),
            pltpu.VMEM((ts, D_MODEL), bf16),
        ] + resident_bf16 + [
            pltpu.VMEM((2 * FFN_CHUNK // LANES, CONV_HIST + ts, LANES), jnp.float32)
            for _ in range(N_FFN_CHUNKS)
        ],
        compiler_params=pltpu.CompilerParams(
            dimension_semantics=("arbitrary",),
            vmem_limit_bytes=V7X_VMEM_LIMIT_BYTES,
        ),
        name="hybrid_block",
    )(*operands)
```

```python
import functools

import jax
import jax.numpy as jnp
from jax import lax
from jax.experimental import pallas as pl
from jax.experimental.pallas import tpu as pltpu

D_MODEL = 1024
N_POOL_GROUPS = 4
POOL_WINDOWS = (2, 4, 8, 16)
POOL_GROUP_DIM = D_MODEL // N_POOL_GROUPS
MXU_TILE = 256
D_FF = 2816
KSIZE = 3
RMS_EPS = 1e-6
LANES = 128
BF16_ROWS = 16

SEQ_TILE = 256
POOL_HIST = 16
CONV_HIST = 8
FFN_CHUNK = 256
N_FFN_CHUNKS = D_FF // FFN_CHUNK
UP_AHEAD = 2
CAST_CHUNK_BYTES = 1024 * 1024
CAST_SLOTS = 6
V7X_VMEM_LIMIT_BYTES = 60 * 1024 * 1024


def _rms_norm(x, g):
    inv = lax.rsqrt(jnp.mean(x * x, axis=-1, keepdims=True) + RMS_EPS)
    return x * inv * g


def _dot(a, b):
    return jnp.dot(a, b, preferred_element_type=jnp.float32)


def _slab_store(buf, row0, val):
    for j in range(buf.shape[0]):
        buf[j, row0:row0 + val.shape[0], :] = val[:, j * LANES:(j + 1) * LANES]


def _slab_rows(buf, j, start, rows):
    return buf[pl.ds(j, 1, stride=2), pl.ds(start, rows), :][0]


def _slab_window(buf, start, rows):
    return jnp.concatenate([_slab_rows(buf, j, start, rows) for j in range(buf.shape[0])], axis=-1)


def _carry_history(buf, hist, rows, keep):
    buf[:, 0:hist, :] = jnp.where(keep, buf[:, rows:rows + hist, :], 0.0)


def _causal_taps(buf, hist, rows, w):
    out = None
    for k in range(KSIZE):
        term = w[k:k + 1, :] * _slab_window(buf, hist - (KSIZE - 1) + k, rows)
        out = term if out is None else out + term
    return out


def _block(j):
    return slice(j * MXU_TILE, (j + 1) * MXU_TILE)


def _tile_pair(x_ref, o_ref, x1_buf, h2_buf, t_tok, t_ch, tiles_per_row,
               g_mix_ref, w_in_ref, w_pool_ref, conv_w_ref,
               w_conv_out_ref, w_o_ref, g_ffn_ref, w_up_ref, ffn_conv_w_ref, ffn_conv_b_ref,
               w_down_ref, g_final_ref, zp_buf, cv_buf, u_bufs):
    ts = x_ref.shape[0]
    bf16 = jnp.bfloat16
    tok_continues = t_tok + 1 < tiles_per_row
    ch_continues = t_ch + 1 < tiles_per_row

    def gate_val(ref, c):
        gate = ref[:, c * FFN_CHUNK:(c + 1) * FFN_CHUNK]
        val = ref[:, D_FF + c * FFN_CHUNK:D_FF + (c + 1) * FFN_CHUNK]
        return jnp.concatenate([gate, val], axis=1)

    def up_proj(c):
        _slab_store(u_bufs[c], CONV_HIST, _dot(h2_buf[...], gate_val(w_up_ref, c)))

    def ffn_act(c):
        u_buf = u_bufs[c]
        u = (_causal_taps(u_buf, CONV_HIST, ts, gate_val(ffn_conv_w_ref, c))
             + gate_val(ffn_conv_b_ref, c))
        _carry_history(u_buf, CONV_HIST, ts, ch_continues)
        return (jax.nn.silu(u[:, :FFN_CHUNK]) * u[:, FFN_CHUNK:]).astype(bf16)

    def down_proj(c, act):
        return _dot(act, w_down_ref[c * FFN_CHUNK:(c + 1) * FFN_CHUNK, :])

    def in_proj(seg):
        return _dot(v["h"], w_in_ref[:, seg * D_MODEL:(seg + 1) * D_MODEL])

    def blocked_dot(lhs_block, w_ref, blocks):
        out = None
        for j in blocks:
            part = _dot(lhs_block(j), w_ref[_block(j), :])
            out = part if out is None else out + part
        return out

    all_blocks = range(D_MODEL // MXU_TILE)

    def pool_windows():
        pos = t_tok * ts + lax.broadcasted_iota(jnp.int32, (ts, 1), 0) + 1
        slabs_per_group = POOL_GROUP_DIM // LANES
        pooled = []
        for gi, win in enumerate(POOL_WINDOWS):
            def window(back):
                return jnp.concatenate(
                    [_slab_rows(zp_buf, gi * slabs_per_group + j, POOL_HIST - back, ts)
                     for j in range(slabs_per_group)], axis=-1)
            ug = window(0)
            s = ug
            for k in range(1, win):
                s = s + window(k)
            inv_count = 1.0 / jnp.minimum(pos, win).astype(jnp.float32)
            pooled.append((s * inv_count - ug).astype(bf16))
        _carry_history(zp_buf, POOL_HIST, ts, tok_continues)
        return pooled

    v = {}

    def t_conv_c():
        v["z_c"] = in_proj(2)

    def t_windows():
        v["pooled"] = pool_windows()

    def t_conv_v():
        _slab_store(cv_buf, CONV_HIST, v["z_c"] * in_proj(3))

    def t_conv_taps():
        v["conv"] = _causal_taps(cv_buf, CONV_HIST, ts, conv_w_ref[...])
        _carry_history(cv_buf, CONV_HIST, ts, tok_continues)

    def t_conv_b():
        v["conv_in"] = (in_proj(1) * v["conv"]).astype(bf16)

    def t_pool_proj():
        v["y_pool"] = blocked_dot(lambda j: v["pooled"][j], w_pool_ref, all_blocks)

    def t_gate_pool():
        v["gated_pool"] = jax.nn.sigmoid(in_proj(4)) * v["y_pool"]

    def t_gate_conv():
        v["gate_conv"] = jax.nn.sigmoid(in_proj(5))

    def t_conv_out():
        y_conv = blocked_dot(lambda j: v["conv_in"][:, _block(j)], w_conv_out_ref, all_blocks)
        v["merged"] = (v["gated_pool"] + v["gate_conv"] * y_conv).astype(bf16)

    def t_out_proj_a():
        v["x1"] = v["x"] + blocked_dot(lambda j: v["merged"][:, _block(j)], w_o_ref, (0, 1))

    def t_out_proj_b():
        x1 = v["x1"] + blocked_dot(lambda j: v["merged"][:, _block(j)], w_o_ref, (2, 3))
        x1_buf[...] = x1
        h2_buf[...] = _rms_norm(x1, g_ffn_ref[...]).astype(bf16)

    def up_head():
        for c in range(UP_AHEAD):
            up_proj(c)

    def token_head():
        v["x"] = x_ref[...]
        v["h"] = _rms_norm(v["x"], g_mix_ref[...]).astype(bf16)
        _slab_store(zp_buf, POOL_HIST, in_proj(0))

    def main():
        token_pieces = (t_conv_c, t_windows, t_conv_v, t_conv_taps, None, t_conv_b,
                        t_pool_proj, t_gate_pool, t_gate_conv, t_conv_out, t_out_proj_a)
        x2 = x1_buf[...]
        for c in range(N_FFN_CHUNKS):
            act = ffn_act(c)
            if c + UP_AHEAD < N_FFN_CHUNKS:
                up_proj(c + UP_AHEAD)
            if token_pieces[c] is not None:
                token_pieces[c]()
            x2 = x2 + down_proj(c, act)
        v["x2"] = x2

    def tail():
        o_ref[...] = _rms_norm(v["x2"], g_final_ref[...])
        t_out_proj_b()

    def pair():
        up_head()
        token_head()
        main()
        tail()

    def token_only():
        token_head()
        for piece in (t_conv_c, t_windows, t_conv_v, t_conv_taps, t_conv_b, t_pool_proj,
                      t_gate_pool, t_gate_conv, t_conv_out, t_out_proj_a, t_out_proj_b):
            piece()

    def channel_only():
        up_head()
        x2 = x1_buf[...]
        for c in range(N_FFN_CHUNKS):
            act = ffn_act(c)
            if c + UP_AHEAD < N_FFN_CHUNKS:
                up_proj(c + UP_AHEAD)
            x2 = x2 + down_proj(c, act)
        o_ref[...] = _rms_norm(x2, g_final_ref[...])

    return pair, token_only, channel_only


def _cast_chunk_rows(rows, cols):
    budget = max(BF16_ROWS, CAST_CHUNK_BYTES // (4 * cols))
    return max(r for r in range(BF16_ROWS, min(rows, budget) + 1, BF16_ROWS) if rows % r == 0)


def _load_as_bf16(src_hbm, dst, chunk_rows):
    rows, cols = src_hbm.shape
    assert rows % chunk_rows == 0 and dst.shape == src_hbm.shape
    n_chunks = rows // chunk_rows
    ahead = CAST_SLOTS - 1

    def scoped(stage, sems):
        def chunk_copy(i, slot):
            return pltpu.make_async_copy(
                src_hbm.at[pl.ds(i * chunk_rows, chunk_rows), :], stage.at[slot], sems.at[slot])

        for k in range(min(ahead, n_chunks)):
            chunk_copy(k, k).start()

        def body(i, carry):
            @pl.when(i + ahead < n_chunks)
            def _():
                chunk_copy(i + ahead, lax.rem(i + ahead, CAST_SLOTS)).start()

            slot = lax.rem(i, CAST_SLOTS)
            chunk_copy(i, slot).wait()
            row0 = pl.multiple_of(i * chunk_rows, chunk_rows)
            dst[pl.ds(row0, chunk_rows), :] = stage[slot].astype(jnp.bfloat16)
            return carry

        lax.fori_loop(0, n_chunks, body, 0)

    pl.run_scoped(scoped, pltpu.VMEM((CAST_SLOTS, chunk_rows, cols), jnp.float32),
                  pltpu.SemaphoreType.DMA((CAST_SLOTS,)))


def _fold_pool_weights(pool_w_hbm, pool_scale_ref, w_pool_proj_hbm, dst):
    def scoped(pool_w, w_pool_proj, sems):
        copies = (pltpu.make_async_copy(pool_w_hbm, pool_w, sems.at[0]),
                  pltpu.make_async_copy(w_pool_proj_hbm, w_pool_proj, sems.at[1]))
        for copy in copies:
            copy.start()
        for copy in copies:
            copy.wait()
        for gi in range(N_POOL_GROUPS):
            scaled = pool_w[_block(gi), :] * pool_scale_ref[:, _block(gi)]
            dst[_block(gi), :] = jnp.dot(
                scaled, w_pool_proj[_block(gi), :], precision=lax.Precision.HIGHEST,
                preferred_element_type=jnp.float32).astype(jnp.bfloat16)

    pl.run_scoped(scoped, pltpu.VMEM(pool_w_hbm.shape, jnp.float32),
                  pltpu.VMEM(w_pool_proj_hbm.shape, jnp.float32), pltpu.SemaphoreType.DMA((2,)))


def _block_kernel(tiles_per_row, n_tiles, x_hbm, *refs):
    (g_mix_ref, w_in_hbm, pool_w_hbm, pool_scale_ref, w_pool_proj_hbm, conv_w_ref,
     w_conv_out_hbm, w_o_hbm, g_ffn_ref, w_up_hbm, ffn_conv_w_ref, ffn_conv_b_ref, w_down_hbm,
     g_final_ref, o_hbm, x_stage, o_stage, io_sems, zp_buf, cv_buf, x1_buf, h2_buf,
     w_in_ref, w_pool_ref, w_conv_out_ref, w_o_ref, w_up_ref, w_down_ref) = refs[:28]
    u_bufs = refs[28:]
    weights = (g_mix_ref, w_in_ref, w_pool_ref, conv_w_ref,
               w_conv_out_ref, w_o_ref, g_ffn_ref, w_up_ref, ffn_conv_w_ref, ffn_conv_b_ref,
               w_down_ref, g_final_ref)
    ts = x_stage.shape[1]

    def tile_of(ref, tile):
        row0 = pl.multiple_of(lax.rem(tile, tiles_per_row) * ts, ts)
        return ref.at[tile // tiles_per_row, pl.ds(row0, ts), :]

    def x_copy(tile, slot):
        return pltpu.make_async_copy(tile_of(x_hbm, tile), x_stage.at[slot], io_sems.at[0, slot])

    def o_copy(tile, slot):
        return pltpu.make_async_copy(o_stage.at[slot], tile_of(o_hbm, tile), io_sems.at[1, slot])

    x_copy(0, 0).start()
    zp_buf[:, 0:POOL_HIST, :] = jnp.zeros((zp_buf.shape[0], POOL_HIST, LANES), jnp.float32)
    for buf in (cv_buf,) + tuple(u_bufs):
        buf[:, 0:CONV_HIST, :] = jnp.zeros((buf.shape[0], CONV_HIST, LANES), jnp.float32)
    for src, dst in ((w_in_hbm, w_in_ref), (w_conv_out_hbm, w_conv_out_ref), (w_o_hbm, w_o_ref),
                     (w_up_hbm, w_up_ref), (w_down_hbm, w_down_ref)):
        _load_as_bf16(src, dst, _cast_chunk_rows(*src.shape))
    _fold_pool_weights(pool_w_hbm, pool_scale_ref, w_pool_proj_hbm, w_pool_ref)

    def phases(slot, tok_tile, ch_tile):
        return _tile_pair(
            x_stage.at[slot], o_stage.at[slot], x1_buf, h2_buf,
            lax.rem(tok_tile, tiles_per_row), lax.rem(ch_tile, tiles_per_row), tiles_per_row,
            *weights, zp_buf, cv_buf, u_bufs)

    x_copy(1, 1).start()
    x_copy(0, 0).wait()
    _, token_only, _ = phases(0, 0, 0)
    token_only()

    def step(i, carry):
        slot = lax.rem(i, 2)

        @pl.when(i + 1 < n_tiles)
        def _():
            x_copy(i + 1, 1 - slot).start()

        x_copy(i, slot).wait()

        @pl.when(i >= 3)
        def _():
            o_copy(i - 3, slot).wait()

        pair, _, _ = phases(slot, i, i - 1)
        pair()
        o_copy(i - 1, slot).start()
        return carry

    lax.fori_loop(1, n_tiles, step, 0)
    last_slot = n_tiles % 2
    o_copy(n_tiles - 3, last_slot).wait()
    _, _, channel_only = phases(last_slot, n_tiles - 1, n_tiles - 1)
    channel_only()
    o_copy(n_tiles - 1, last_slot).start()
    o_copy(n_tiles - 2, 1 - last_slot).wait()
    o_copy(n_tiles - 1, last_slot).wait()


def _whole(shape):
    block = shape if len(shape) == 2 else (None,) + tuple(shape[1:])
    return pl.BlockSpec(block, lambda i: (0,) * len(shape))


@jax.jit
def kernel(x, norm_mix, w_in, pool_w, pool_scale, w_pool_proj, conv_w, w_conv_out, w_o,
           norm_ffn, w_up, ffn_conv_w, ffn_conv_b, w_down, norm_final):
    batch, seq, d_model = x.shape
    ts = SEQ_TILE
    assert d_model == D_MODEL and seq % ts == 0 and D_FF % FFN_CHUNK == 0
    assert POOL_GROUP_DIM == MXU_TILE and N_FFN_CHUNKS == 11
    assert norm_mix.shape[0] == 1, "single-layer block"
    bf16 = jnp.bfloat16
    tiles_per_row = seq // ts
    n_tiles = batch * tiles_per_row
    assert n_tiles >= 3, "the output-copy bookkeeping assumes at least three tiles"

    matmul_weights = {2, 3, 5, 7, 8, 10, 13}
    operands = (
        x,
        norm_mix[0][None, :],
        w_in[0],
        pool_w[0].reshape(N_POOL_GROUPS * POOL_GROUP_DIM, POOL_GROUP_DIM),
        pool_scale[0][None, :],
        w_pool_proj[0],
        conv_w,
        w_conv_out[0],
        w_o[0],
        norm_ffn[0][None, :],
        w_up[0],
        ffn_conv_w,
        ffn_conv_b[0][None, :],
        w_down[0],
        norm_final[None, :],
    )

    any_spec = pl.BlockSpec(memory_space=pl.ANY)
    in_specs = [any_spec] + [
        any_spec if k in matmul_weights else _whole(op.shape)
        for k, op in enumerate(operands) if k > 0]
    resident_bf16 = [pltpu.VMEM(operands[k].shape, bf16) for k in (2, 5, 7, 8, 10, 13)]

    return pl.pallas_call(
        functools.partial(_block_kernel, tiles_per_row, n_tiles),
        grid=(1,),
        in_specs=in_specs,
        out_specs=any_spec,
        out_shape=jax.ShapeDtypeStruct(x.shape, x.dtype),
        scratch_shapes=[
            pltpu.VMEM((2, ts, D_MODEL), jnp.float32),
            pltpu.VMEM((2, ts, D_MODEL), jnp.float32),
            pltpu.SemaphoreType.DMA((2, 2)),
            pltpu.VMEM((D_MODEL // LANES, POOL_HIST + ts, LANES), jnp.float32),
            pltpu.VMEM((D_MODEL // LANES, CONV_HIST + ts, LANES), jnp.float32),
            pltpu.VMEM((ts, D_MODEL), jnp.float32),
            pltpu.VMEM((ts, D_MODEL), bf16),
        ] + resident_bf16 + [
            pltpu.VMEM((2 * FFN_CHUNK // LANES, CONV_HIST + ts, LANES), jnp.float32)
            for _ in range(N_FFN_CHUNKS)
        ],
        compiler_params=pltpu.CompilerParams(
            dimension_semantics=("arbitrary",),
            vmem_limit_bytes=V7X_VMEM_LIMIT_BYTES,
        ),
        name="hybrid_block",
    )(*operands)
```

```python
import functools

import jax
import jax.numpy as jnp
from jax import lax
from jax.experimental import pallas as pl
from jax.experimental.pallas import tpu as pltpu

D_MODEL = 1024
N_POOL_GROUPS = 4
POOL_WINDOWS = (2, 4, 8, 16)
POOL_GROUP_DIM = D_MODEL // N_POOL_GROUPS
MXU_TILE = 256
D_FF = 2816
KSIZE = 3
RMS_EPS = 1e-6
LANES = 128
BF16_ROWS = 16

SEQ_TILE = 256
POOL_HIST = 16
CONV_HIST = 8
FFN_CHUNK = 256
N_FFN_CHUNKS = D_FF // FFN_CHUNK
UP_AHEAD = 2
CAST_CHUNK_BYTES = 768 * 1024
CAST_SLOTS = 8
V7X_VMEM_LIMIT_BYTES = 60 * 1024 * 1024


def _rms_norm(x, g):
    inv = lax.rsqrt(jnp.mean(x * x, axis=-1, keepdims=True) + RMS_EPS)
    return x * inv * g


def _dot(a, b):
    return jnp.dot(a, b, preferred_element_type=jnp.float32)


def _slab_store(buf, row0, val):
    for j in range(buf.shape[0]):
        buf[j, row0:row0 + val.shape[0], :] = val[:, j * LANES:(j + 1) * LANES]


def _slab_rows(buf, j, start, rows):
    return buf[pl.ds(j, 1, stride=2), pl.ds(start, rows), :][0]


def _slab_window(buf, start, rows):
    return jnp.concatenate([_slab_rows(buf, j, start, rows) for j in range(buf.shape[0])], axis=-1)


def _carry_history(buf, hist, rows, keep):
    buf[:, 0:hist, :] = jnp.where(keep, buf[:, rows:rows + hist, :], 0.0)


def _causal_taps(buf, hist, rows, w):
    out = None
    for k in range(KSIZE):
        term = w[k:k + 1, :] * _slab_window(buf, hist - (KSIZE - 1) + k, rows)
        out = term if out is None else out + term
    return out


def _block(j):
    return slice(j * MXU_TILE, (j + 1) * MXU_TILE)


def _tile_pair(x_ref, o_ref, x1_buf, h2_buf, t_tok, t_ch, tiles_per_row,
               g_mix_ref, w_in_ref, w_pool_ref, conv_w_ref,
               w_conv_out_ref, w_o_ref, g_ffn_ref, w_up_ref, ffn_conv_w_ref, ffn_conv_b_ref,
               w_down_ref, g_final_ref, zp_buf, cv_buf, u_bufs):
    ts = x_ref.shape[0]
    bf16 = jnp.bfloat16
    tok_continues = t_tok + 1 < tiles_per_row
    ch_continues = t_ch + 1 < tiles_per_row

    def gate_val(ref, c):
        gate = ref[:, c * FFN_CHUNK:(c + 1) * FFN_CHUNK]
        val = ref[:, D_FF + c * FFN_CHUNK:D_FF + (c + 1) * FFN_CHUNK]
        return jnp.concatenate([gate, val], axis=1)

    def up_proj(c):
        _slab_store(u_bufs[c], CONV_HIST, _dot(h2_buf[...], gate_val(w_up_ref, c)))

    def ffn_act(c):
        u_buf = u_bufs[c]
        u = (_causal_taps(u_buf, CONV_HIST, ts, gate_val(ffn_conv_w_ref, c))
             + gate_val(ffn_conv_b_ref, c))
        _carry_history(u_buf, CONV_HIST, ts, ch_continues)
        return (jax.nn.silu(u[:, :FFN_CHUNK]) * u[:, FFN_CHUNK:]).astype(bf16)

    def down_proj(c, act):
        return _dot(act, w_down_ref[c * FFN_CHUNK:(c + 1) * FFN_CHUNK, :])

    def in_proj(seg):
        return _dot(v["h"], w_in_ref[:, seg * D_MODEL:(seg + 1) * D_MODEL])

    def blocked_dot(lhs_block, w_ref, blocks):
        out = None
        for j in blocks:
            part = _dot(lhs_block(j), w_ref[_block(j), :])
            out = part if out is None else out + part
        return out

    all_blocks = range(D_MODEL // MXU_TILE)

    def pool_windows():
        pos = t_tok * ts + lax.broadcasted_iota(jnp.int32, (ts, 1), 0) + 1
        slabs_per_group = POOL_GROUP_DIM // LANES
        pooled = []
        for gi, win in enumerate(POOL_WINDOWS):
            def window(back):
                return jnp.concatenate(
                    [_slab_rows(zp_buf, gi * slabs_per_group + j, POOL_HIST - back, ts)
                     for j in range(slabs_per_group)], axis=-1)
            ug = window(0)
            s = ug
            for k in range(1, win):
                s = s + window(k)
            inv_count = 1.0 / jnp.minimum(pos, win).astype(jnp.float32)
            pooled.append((s * inv_count - ug).astype(bf16))
        _carry_history(zp_buf, POOL_HIST, ts, tok_continues)
        return pooled

    v = {}

    def t_conv_c():
        v["z_c"] = in_proj(2)

    def t_windows():
        v["pooled"] = pool_windows()

    def t_conv_v():
        _slab_store(cv_buf, CONV_HIST, v["z_c"] * in_proj(3))

    def t_conv_taps():
        v["conv"] = _causal_taps(cv_buf, CONV_HIST, ts, conv_w_ref[...])
        _carry_history(cv_buf, CONV_HIST, ts, tok_continues)

    def t_conv_b():
        v["conv_in"] = (in_proj(1) * v["conv"]).astype(bf16)

    def t_pool_proj():
        v["y_pool"] = blocked_dot(lambda j: v["pooled"][j], w_pool_ref, all_blocks)

    def t_gate_pool():
        v["gated_pool"] = jax.nn.sigmoid(in_proj(4)) * v["y_pool"]

    def t_gate_conv():
        v["gate_conv"] = jax.nn.sigmoid(in_proj(5))

    def t_conv_out():
        y_conv = blocked_dot(lambda j: v["conv_in"][:, _block(j)], w_conv_out_ref, all_blocks)
        v["merged"] = (v["gated_pool"] + v["gate_conv"] * y_conv).astype(bf16)

    def t_out_proj_a():
        v["x1"] = v["x"] + blocked_dot(lambda j: v["merged"][:, _block(j)], w_o_ref, (0, 1))

    def t_out_proj_b():
        x1 = v["x1"] + blocked_dot(lambda j: v["merged"][:, _block(j)], w_o_ref, (2, 3))
        x1_buf[...] = x1
        h2_buf[...] = _rms_norm(x1, g_ffn_ref[...]).astype(bf16)

    def up_head():
        for c in range(UP_AHEAD):
            up_proj(c)

    def token_head():
        v["x"] = x_ref[...]
        v["h"] = _rms_norm(v["x"], g_mix_ref[...]).astype(bf16)
        _slab_store(zp_buf, POOL_HIST, in_proj(0))

    def main():
        token_pieces = (t_conv_c, t_windows, t_conv_v, t_conv_taps, None, t_conv_b,
                        t_pool_proj, t_gate_pool, t_gate_conv, t_conv_out, t_out_proj_a)
        x2 = x1_buf[...]
        for c in range(N_FFN_CHUNKS):
            act = ffn_act(c)
            if c + UP_AHEAD < N_FFN_CHUNKS:
                up_proj(c + UP_AHEAD)
            if token_pieces[c] is not None:
                token_pieces[c]()
            x2 = x2 + down_proj(c, act)
        v["x2"] = x2

    def tail():
        o_ref[...] = _rms_norm(v["x2"], g_final_ref[...])
        t_out_proj_b()

    def pair():
        up_head()
        token_head()
        main()
        tail()

    def token_only():
        token_head()
        for piece in (t_conv_c, t_windows, t_conv_v, t_conv_taps, t_conv_b, t_pool_proj,
                      t_gate_pool, t_gate_conv, t_conv_out, t_out_proj_a, t_out_proj_b):
            piece()

    def channel_only():
        up_head()
        x2 = x1_buf[...]
        for c in range(N_FFN_CHUNKS):
            act = ffn_act(c)
            if c + UP_AHEAD < N_FFN_CHUNKS:
                up_proj(c + UP_AHEAD)
            x2 = x2 + down_proj(c, act)
        o_ref[...] = _rms_norm(x2, g_final_ref[...])

    return pair, token_only, channel_only


def _cast_chunk_rows(rows, cols):
    budget = max(BF16_ROWS, CAST_CHUNK_BYTES // (4 * cols))
    return max(r for r in range(BF16_ROWS, min(rows, budget) + 1, BF16_ROWS) if rows % r == 0)


def _load_as_bf16(src_hbm, dst, chunk_rows):
    rows, cols = src_hbm.shape
    assert rows % chunk_rows == 0 and dst.shape == src_hbm.shape
    n_chunks = rows // chunk_rows
    ahead = CAST_SLOTS - 1

    def scoped(stage, sems):
        def chunk_copy(i, slot):
            return pltpu.make_async_copy(
                src_hbm.at[pl.ds(i * chunk_rows, chunk_rows), :], stage.at[slot], sems.at[slot])

        for k in range(min(ahead, n_chunks)):
            chunk_copy(k, k).start()

        def body(i, carry):
            @pl.when(i + ahead < n_chunks)
            def _():
                chunk_copy(i + ahead, lax.rem(i + ahead, CAST_SLOTS)).start()

            slot = lax.rem(i, CAST_SLOTS)
            chunk_copy(i, slot).wait()
            row0 = pl.multiple_of(i * chunk_rows, chunk_rows)
            dst[pl.ds(row0, chunk_rows), :] = stage[slot].astype(jnp.bfloat16)
            return carry

        lax.fori_loop(0, n_chunks, body, 0)

    pl.run_scoped(scoped, pltpu.VMEM((CAST_SLOTS, chunk_rows, cols), jnp.float32),
                  pltpu.SemaphoreType.DMA((CAST_SLOTS,)))


def _fold_pool_weights(pool_w_hbm, pool_scale_ref, w_pool_proj_hbm, dst):
    def scoped(pool_w, w_pool_proj, sems):
        copies = (pltpu.make_async_copy(pool_w_hbm, pool_w, sems.at[0]),
                  pltpu.make_async_copy(w_pool_proj_hbm, w_pool_proj, sems.at[1]))
        for copy in copies:
            copy.start()
        for copy in copies:
            copy.wait()
        for gi in range(N_POOL_GROUPS):
            scaled = pool_w[_block(gi), :] * pool_scale_ref[:, _block(gi)]
            dst[_block(gi), :] = jnp.dot(
                scaled, w_pool_proj[_block(gi), :], precision=lax.Precision.HIGHEST,
                preferred_element_type=jnp.float32).astype(jnp.bfloat16)

    pl.run_scoped(scoped, pltpu.VMEM(pool_w_hbm.shape, jnp.float32),
                  pltpu.VMEM(w_pool_proj_hbm.shape, jnp.float32), pltpu.SemaphoreType.DMA((2,)))


def _block_kernel(tiles_per_row, n_tiles, x_hbm, *refs):
    (g_mix_ref, w_in_hbm, pool_w_hbm, pool_scale_ref, w_pool_proj_hbm, conv_w_ref,
     w_conv_out_hbm, w_o_hbm, g_ffn_ref, w_up_hbm, ffn_conv_w_ref, ffn_conv_b_ref, w_down_hbm,
     g_final_ref, o_hbm, x_stage, o_stage, io_sems, zp_buf, cv_buf, x1_buf, h2_buf,
     w_in_ref, w_pool_ref, w_conv_out_ref, w_o_ref, w_up_ref, w_down_ref) = refs[:28]
    u_bufs = refs[28:]
    weights = (g_mix_ref, w_in_ref, w_pool_ref, conv_w_ref,
               w_conv_out_ref, w_o_ref, g_ffn_ref, w_up_ref, ffn_conv_w_ref, ffn_conv_b_ref,
               w_down_ref, g_final_ref)
    ts = x_stage.shape[1]

    def tile_of(ref, tile):
        row0 = pl.multiple_of(lax.rem(tile, tiles_per_row) * ts, ts)
        return ref.at[tile // tiles_per_row, pl.ds(row0, ts), :]

    def x_copy(tile, slot):
        return pltpu.make_async_copy(tile_of(x_hbm, tile), x_stage.at[slot], io_sems.at[0, slot])

    def o_copy(tile, slot):
        return pltpu.make_async_copy(o_stage.at[slot], tile_of(o_hbm, tile), io_sems.at[1, slot])

    x_copy(0, 0).start()
    zp_buf[:, 0:POOL_HIST, :] = jnp.zeros((zp_buf.shape[0], POOL_HIST, LANES), jnp.float32)
    for buf in (cv_buf,) + tuple(u_bufs):
        buf[:, 0:CONV_HIST, :] = jnp.zeros((buf.shape[0], CONV_HIST, LANES), jnp.float32)
    for src, dst in ((w_in_hbm, w_in_ref), (w_conv_out_hbm, w_conv_out_ref), (w_o_hbm, w_o_ref),
                     (w_up_hbm, w_up_ref), (w_down_hbm, w_down_ref)):
        _load_as_bf16(src, dst, _cast_chunk_rows(*src.shape))
    _fold_pool_weights(pool_w_hbm, pool_scale_ref, w_pool_proj_hbm, w_pool_ref)

    def phases(slot, tok_tile, ch_tile):
        return _tile_pair(
            x_stage.at[slot], o_stage.at[slot], x1_buf, h2_buf,
            lax.rem(tok_tile, tiles_per_row), lax.rem(ch_tile, tiles_per_row), tiles_per_row,
            *weights, zp_buf, cv_buf, u_bufs)

    x_copy(1, 1).start()
    x_copy(0, 0).wait()
    _, token_only, _ = phases(0, 0, 0)
    token_only()

    def step(i, carry):
        slot = lax.rem(i, 2)

        @pl.when(i + 1 < n_tiles)
        def _():
            x_copy(i + 1, 1 - slot).start()

        x_copy(i, slot).wait()

        @pl.when(i >= 3)
        def _():
            o_copy(i - 3, slot).wait()

        pair, _, _ = phases(slot, i, i - 1)
        pair()
        o_copy(i - 1, slot).start()
        return carry

    lax.fori_loop(1, n_tiles, step, 0)
    last_slot = n_tiles % 2
    o_copy(n_tiles - 3, last_slot).wait()
    _, _, channel_only = phases(last_slot, n_tiles - 1, n_tiles - 1)
    channel_only()
    o_copy(n_tiles - 1, last_slot).start()
    o_copy(n_tiles - 2, 1 - last_slot).wait()
    o_copy(n_tiles - 1, last_slot).wait()


def _whole(shape):
    return pl.BlockSpec(shape, lambda i: (0,) * len(shape))


@jax.jit
def kernel(x, norm_mix, w_in, pool_w, pool_scale, w_pool_proj, conv_w, w_conv_out, w_o,
           norm_ffn, w_up, ffn_conv_w, ffn_conv_b, w_down, norm_final):
    batch, seq, d_model = x.shape
    ts = SEQ_TILE
    assert d_model == D_MODEL and seq % ts == 0 and D_FF % FFN_CHUNK == 0
    assert POOL_GROUP_DIM == MXU_TILE and N_FFN_CHUNKS == 11
    assert norm_mix.shape[0] == 1, "single-layer block"
    bf16 = jnp.bfloat16
    tiles_per_row = seq // ts
    n_tiles = batch * tiles_per_row
    assert n_tiles >= 3, "the output-copy bookkeeping assumes at least three tiles"

    matmul_weights = {2, 3, 5, 7, 8, 10, 13}
    operands = (
        x,
        norm_mix[0][None, :],
        w_in[0],
        pool_w[0].reshape(N_POOL_GROUPS * POOL_GROUP_DIM, POOL_GROUP_DIM),
        pool_scale[0][None, :],
        w_pool_proj[0],
        conv_w[0],
        w_conv_out[0],
        w_o[0],
        norm_ffn[0][None, :],
        w_up[0],
        ffn_conv_w[0],
        ffn_conv_b[0][None, :],
        w_down[0],
        norm_final[None, :],
    )

    any_spec = pl.BlockSpec(memory_space=pl.ANY)
    in_specs = [any_spec] + [
        any_spec if k in matmul_weights else _whole(op.shape)
        for k, op in enumerate(operands) if k > 0]
    resident_bf16 = [pltpu.VMEM(operands[k].shape, bf16) for k in (2, 5, 7, 8, 10, 13)]

    return pl.pallas_call(
        functools.partial(_block_kernel, tiles_per_row, n_tiles),
        grid=(1,),
        in_specs=in_specs,
        out_specs=any_spec,
        out_shape=jax.ShapeDtypeStruct(x.shape, x.dtype),
        scratch_shapes=[
            pltpu.VMEM((2, ts, D_MODEL), jnp.float32),
            pltpu.VMEM((2, ts, D_MODEL), jnp.float32),
            pltpu.SemaphoreType.DMA((2, 2)),
            pltpu.VMEM((D_MODEL // LANES, POOL_HIST + ts, LANES), jnp.float32),
            pltpu.VMEM((D_MODEL // LANES, CONV_HIST + ts, LANES), jnp.float32),
            pltpu.VMEM((ts, D_MODEL), jnp.float32),
            pltpu.VMEM((ts, D_MODEL), bf16),
        ] + resident_bf16 + [
            pltpu.VMEM((2 * FFN_CHUNK // LANES, CONV_HIST + ts, LANES), jnp.float32)
            for _ in range(N_FFN_CHUNKS)
        ],
        compiler_params=pltpu.CompilerParams(
            dimension_semantics=("arbitrary",),
            vmem_limit_bytes=V7X_VMEM_LIMIT_BYTES,
        ),
        name="hybrid_block",
    )(*operands)
```

```python
import functools

import jax
import jax.numpy as jnp
from jax import lax
from jax.experimental import pallas as pl
from jax.experimental.pallas import tpu as pltpu

D_MODEL = 1024
N_POOL_GROUPS = 4
POOL_WINDOWS = (2, 4, 8, 16)
POOL_GROUP_DIM = D_MODEL // N_POOL_GROUPS
MXU_TILE = 256
D_FF = 2816
KSIZE = 3
RMS_EPS = 1e-6
LANES = 128
BF16_ROWS = 16

SEQ_TILE = 256
POOL_HIST = 16
CONV_HIST = 8
FFN_CHUNK = 256
N_FFN_CHUNKS = D_FF // FFN_CHUNK
UP_AHEAD = 2
CAST_CHUNK_BYTES = 1024 * 1024
CAST_SLOTS = 6
V7X_VMEM_LIMIT_BYTES = 60 * 1024 * 1024


def _rms_norm(x, g):
    inv = lax.rsqrt(jnp.mean(x * x, axis=-1, keepdims=True) + RMS_EPS)
    return x * inv * g


def _dot(a, b):
    return jnp.dot(a, b, preferred_element_type=jnp.float32)


def _slab_store(buf, row0, val):
    for j in range(buf.shape[0]):
        buf[j, row0:row0 + val.shape[0], :] = val[:, j * LANES:(j + 1) * LANES]


def _slab_rows(buf, j, start, rows):
    return buf[pl.ds(j, 1, stride=2), pl.ds(start, rows), :][0]


def _slab_window(buf, start, rows):
    return jnp.concatenate([_slab_rows(buf, j, start, rows) for j in range(buf.shape[0])], axis=-1)


def _carry_history(buf, hist, rows, keep):
    buf[:, 0:hist, :] = jnp.where(keep, buf[:, rows:rows + hist, :], 0.0)


def _causal_taps(buf, hist, rows, w):
    out = None
    for k in range(KSIZE):
        term = w[k:k + 1, :] * _slab_window(buf, hist - (KSIZE - 1) + k, rows)
        out = term if out is None else out + term
    return out


def _block(j):
    return slice(j * MXU_TILE, (j + 1) * MXU_TILE)


def _tile_pair(x_ref, o_ref, x1_buf, h2_buf, t_tok, t_ch, tiles_per_row,
               g_mix_ref, w_in_ref, w_pool_ref, conv_w_ref,
               w_conv_out_ref, w_o_ref, g_ffn_ref, w_up_ref, ffn_conv_w_ref, ffn_conv_b_ref,
               w_down_ref, g_final_ref, zp_buf, cv_buf, u_bufs):
    ts = x_ref.shape[0]
    bf16 = jnp.bfloat16
    tok_continues = t_tok + 1 < tiles_per_row
    ch_continues = t_ch + 1 < tiles_per_row

    def gate_val(ref, c):
        gate = ref[:, c * FFN_CHUNK:(c + 1) * FFN_CHUNK]
        val = ref[:, D_FF + c * FFN_CHUNK:D_FF + (c + 1) * FFN_CHUNK]
        return jnp.concatenate([gate, val], axis=1)

    def up_proj(c):
        u = _dot(h2_buf[...], gate_val(w_up_ref, c))
        _slab_store(u_bufs[c], CONV_HIST, u * jnp.tile(v["inv_rms"], (1, u.shape[1] // LANES)))

    def ffn_act(c):
        u_buf = u_bufs[c]
        u = (_causal_taps(u_buf, CONV_HIST, ts, gate_val(ffn_conv_w_ref, c))
             + gate_val(ffn_conv_b_ref, c))
        _carry_history(u_buf, CONV_HIST, ts, ch_continues)
        return (jax.nn.silu(u[:, :FFN_CHUNK]) * u[:, FFN_CHUNK:]).astype(bf16)

    def down_proj(c, act):
        return _dot(act, w_down_ref[c * FFN_CHUNK:(c + 1) * FFN_CHUNK, :])

    def in_proj(seg):
        return _dot(v["h"], w_in_ref[:, seg * D_MODEL:(seg + 1) * D_MODEL])

    def blocked_dot(lhs_block, w_ref, blocks):
        out = None
        for j in blocks:
            part = _dot(lhs_block(j), w_ref[_block(j), :])
            out = part if out is None else out + part
        return out

    all_blocks = range(D_MODEL // MXU_TILE)

    def pool_windows():
        pos = t_tok * ts + lax.broadcasted_iota(jnp.int32, (ts, 1), 0) + 1
        slabs_per_group = POOL_GROUP_DIM // LANES
        pooled = []
        for gi, win in enumerate(POOL_WINDOWS):
            def window(back):
                return jnp.concatenate(
                    [_slab_rows(zp_buf, gi * slabs_per_group + j, POOL_HIST - back, ts)
                     for j in range(slabs_per_group)], axis=-1)
            ug = window(0)
            s = ug
            for k in range(1, win):
                s = s + window(k)
            inv_count = 1.0 / jnp.minimum(pos, win).astype(jnp.float32)
            pooled.append((s * inv_count - ug).astype(bf16))
        _carry_history(zp_buf, POOL_HIST, ts, tok_continues)
        return pooled

    v = {}

    def t_conv_c():
        v["z_c"] = in_proj(2)

    def t_windows():
        v["pooled"] = pool_windows()

    def t_conv_v():
        _slab_store(cv_buf, CONV_HIST, v["z_c"] * in_proj(3))

    def t_conv_taps():
        v["conv"] = _causal_taps(cv_buf, CONV_HIST, ts, conv_w_ref[...])
        _carry_history(cv_buf, CONV_HIST, ts, tok_continues)

    def t_conv_b():
        v["conv_in"] = (in_proj(1) * v["conv"]).astype(bf16)

    def t_pool_proj():
        v["y_pool"] = blocked_dot(lambda j: v["pooled"][j], w_pool_ref, all_blocks)

    def t_gate_pool():
        v["gated_pool"] = jax.nn.sigmoid(in_proj(4)) * v["y_pool"]

    def t_gate_conv():
        v["gate_conv"] = jax.nn.sigmoid(in_proj(5))

    def t_conv_out():
        y_conv = blocked_dot(lambda j: v["conv_in"][:, _block(j)], w_conv_out_ref, all_blocks)
        v["merged"] = (v["gated_pool"] + v["gate_conv"] * y_conv).astype(bf16)

    def t_out_proj_a():
        v["x1"] = v["x"] + blocked_dot(lambda j: v["merged"][:, _block(j)], w_o_ref, (0, 1))

    def t_out_proj_b():
        x1 = v["x1"] + blocked_dot(lambda j: v["merged"][:, _block(j)], w_o_ref, (2, 3))
        x1_buf[...] = x1
        h2_buf[...] = (x1 * g_ffn_ref[...]).astype(bf16)

    def up_head():
        x1 = x1_buf[...]
        inv = lax.rsqrt(jnp.mean(x1 * x1, axis=-1, keepdims=True) + RMS_EPS)
        v["inv_rms"] = jnp.broadcast_to(inv, (ts, LANES))
        for c in range(UP_AHEAD):
            up_proj(c)

    def token_head():
        v["x"] = x_ref[...]
        v["h"] = _rms_norm(v["x"], g_mix_ref[...]).astype(bf16)
        _slab_store(zp_buf, POOL_HIST, in_proj(0))

    def main():
        token_pieces = (t_conv_c, t_windows, t_conv_v, t_conv_taps, None, t_conv_b,
                        t_pool_proj, t_gate_pool, t_gate_conv, t_conv_out, t_out_proj_a)
        x2 = x1_buf[...]
        for c in range(N_FFN_CHUNKS):
            act = ffn_act(c)
            if c + UP_AHEAD < N_FFN_CHUNKS:
                up_proj(c + UP_AHEAD)
            if token_pieces[c] is not None:
                token_pieces[c]()
            x2 = x2 + down_proj(c, act)
        v["x2"] = x2

    def tail():
        o_ref[...] = _rms_norm(v["x2"], g_final_ref[...])
        t_out_proj_b()

    def pair():
        up_head()
        token_head()
        main()
        tail()

    def token_only():
        token_head()
        for piece in (t_conv_c, t_windows, t_conv_v, t_conv_taps, t_conv_b, t_pool_proj,
                      t_gate_pool, t_gate_conv, t_conv_out, t_out_proj_a, t_out_proj_b):
            piece()

    def channel_only():
        up_head()
        x2 = x1_buf[...]
        for c in range(N_FFN_CHUNKS):
            act = ffn_act(c)
            if c + UP_AHEAD < N_FFN_CHUNKS:
                up_proj(c + UP_AHEAD)
            x2 = x2 + down_proj(c, act)
        o_ref[...] = _rms_norm(x2, g_final_ref[...])

    return pair, token_only, channel_only


def _cast_chunk_rows(rows, cols):
    budget = max(BF16_ROWS, CAST_CHUNK_BYTES // (4 * cols))
    return max(r for r in range(BF16_ROWS, min(rows, budget) + 1, BF16_ROWS) if rows % r == 0)


def _load_as_bf16(src_hbm, dst, chunk_rows):
    rows, cols = src_hbm.shape
    assert rows % chunk_rows == 0 and dst.shape == src_hbm.shape
    n_chunks = rows // chunk_rows
    ahead = CAST_SLOTS - 1

    def scoped(stage, sems):
        def chunk_copy(i, slot):
            return pltpu.make_async_copy(
                src_hbm.at[pl.ds(i * chunk_rows, chunk_rows), :], stage.at[slot], sems.at[slot])

        for k in range(min(ahead, n_chunks)):
            chunk_copy(k, k).start()

        def body(i, carry):
            @pl.when(i + ahead < n_chunks)
            def _():
                chunk_copy(i + ahead, lax.rem(i + ahead, CAST_SLOTS)).start()

            slot = lax.rem(i, CAST_SLOTS)
            chunk_copy(i, slot).wait()
            row0 = pl.multiple_of(i * chunk_rows, chunk_rows)
            dst[pl.ds(row0, chunk_rows), :] = stage[slot].astype(jnp.bfloat16)
            return carry

        lax.fori_loop(0, n_chunks, body, 0)

    pl.run_scoped(scoped, pltpu.VMEM((CAST_SLOTS, chunk_rows, cols), jnp.float32),
                  pltpu.SemaphoreType.DMA((CAST_SLOTS,)))


def _fold_pool_weights(pool_w_hbm, pool_scale_ref, w_pool_proj_hbm, dst):
    def scoped(pool_w, w_pool_proj, sems):
        copies = (pltpu.make_async_copy(pool_w_hbm, pool_w, sems.at[0]),
                  pltpu.make_async_copy(w_pool_proj_hbm, w_pool_proj, sems.at[1]))
        for copy in copies:
            copy.start()
        for copy in copies:
            copy.wait()
        for gi in range(N_POOL_GROUPS):
            scaled = pool_w[_block(gi), :] * pool_scale_ref[:, _block(gi)]
            dst[_block(gi), :] = jnp.dot(
                scaled, w_pool_proj[_block(gi), :], precision=lax.Precision.HIGHEST,
                preferred_element_type=jnp.float32).astype(jnp.bfloat16)

    pl.run_scoped(scoped, pltpu.VMEM(pool_w_hbm.shape, jnp.float32),
                  pltpu.VMEM(w_pool_proj_hbm.shape, jnp.float32), pltpu.SemaphoreType.DMA((2,)))


def _block_kernel(tiles_per_row, n_tiles, x_hbm, *refs):
    (g_mix_ref, w_in_hbm, pool_w_hbm, pool_scale_ref, w_pool_proj_hbm, conv_w_ref,
     w_conv_out_hbm, w_o_hbm, g_ffn_ref, w_up_hbm, ffn_conv_w_ref, ffn_conv_b_ref, w_down_hbm,
     g_final_ref, o_hbm, x_stage, o_stage, io_sems, zp_buf, cv_buf, x1_buf, h2_buf,
     w_in_ref, w_pool_ref, w_conv_out_ref, w_o_ref, w_up_ref, w_down_ref) = refs[:28]
    u_bufs = refs[28:]
    weights = (g_mix_ref, w_in_ref, w_pool_ref, conv_w_ref,
               w_conv_out_ref, w_o_ref, g_ffn_ref, w_up_ref, ffn_conv_w_ref, ffn_conv_b_ref,
               w_down_ref, g_final_ref)
    ts = x_stage.shape[1]

    def tile_of(ref, tile):
        row0 = pl.multiple_of(lax.rem(tile, tiles_per_row) * ts, ts)
        return ref.at[tile // tiles_per_row, pl.ds(row0, ts), :]

    def x_copy(tile, slot):
        return pltpu.make_async_copy(tile_of(x_hbm, tile), x_stage.at[slot], io_sems.at[0, slot])

    def o_copy(tile, slot):
        return pltpu.make_async_copy(o_stage.at[slot], tile_of(o_hbm, tile), io_sems.at[1, slot])

    x_copy(0, 0).start()
    zp_buf[:, 0:POOL_HIST, :] = jnp.zeros((zp_buf.shape[0], POOL_HIST, LANES), jnp.float32)
    for buf in (cv_buf,) + tuple(u_bufs):
        buf[:, 0:CONV_HIST, :] = jnp.zeros((buf.shape[0], CONV_HIST, LANES), jnp.float32)
    for src, dst in ((w_in_hbm, w_in_ref), (w_conv_out_hbm, w_conv_out_ref), (w_o_hbm, w_o_ref),
                     (w_up_hbm, w_up_ref), (w_down_hbm, w_down_ref)):
        _load_as_bf16(src, dst, _cast_chunk_rows(*src.shape))
    _fold_pool_weights(pool_w_hbm, pool_scale_ref, w_pool_proj_hbm, w_pool_ref)

    def phases(slot, tok_tile, ch_tile):
        return _tile_pair(
            x_stage.at[slot], o_stage.at[slot], x1_buf, h2_buf,
            lax.rem(tok_tile, tiles_per_row), lax.rem(ch_tile, tiles_per_row), tiles_per_row,
            *weights, zp_buf, cv_buf, u_bufs)

    x_copy(1, 1).start()
    x_copy(0, 0).wait()
    _, token_only, _ = phases(0, 0, 0)
    token_only()

    def step(i, carry):
        slot = lax.rem(i, 2)

        @pl.when(i + 1 < n_tiles)
        def _():
            x_copy(i + 1, 1 - slot).start()

        x_copy(i, slot).wait()

        @pl.when(i >= 3)
        def _():
            o_copy(i - 3, slot).wait()

        pair, _, _ = phases(slot, i, i - 1)
        pair()
        o_copy(i - 1, slot).start()
        return carry

    lax.fori_loop(1, n_tiles, step, 0)
    last_slot = n_tiles % 2
    o_copy(n_tiles - 3, last_slot).wait()
    _, _, channel_only = phases(last_slot, n_tiles - 1, n_tiles - 1)
    channel_only()
    o_copy(n_tiles - 1, last_slot).start()
    o_copy(n_tiles - 2, 1 - last_slot).wait()
    o_copy(n_tiles - 1, last_slot).wait()


def _whole(shape):
    return pl.BlockSpec(shape, lambda i: (0,) * len(shape))


@jax.jit
def kernel(x, norm_mix, w_in, pool_w, pool_scale, w_pool_proj, conv_w, w_conv_out, w_o,
           norm_ffn, w_up, ffn_conv_w, ffn_conv_b, w_down, norm_final):
    batch, seq, d_model = x.shape
    ts = SEQ_TILE
    assert d_model == D_MODEL and seq % ts == 0 and D_FF % FFN_CHUNK == 0
    assert POOL_GROUP_DIM == MXU_TILE and N_FFN_CHUNKS == 11
    assert norm_mix.shape[0] == 1, "single-layer block"
    bf16 = jnp.bfloat16
    tiles_per_row = seq // ts
    n_tiles = batch * tiles_per_row
    assert n_tiles >= 3, "the output-copy bookkeeping assumes at least three tiles"

    matmul_weights = {2, 3, 5, 7, 8, 10, 13}
    operands = (
        x,
        norm_mix[0][None, :],
        w_in[0],
        pool_w[0].reshape(N_POOL_GROUPS * POOL_GROUP_DIM, POOL_GROUP_DIM),
        pool_scale[0][None, :],
        w_pool_proj[0],
        conv_w[0],
        w_conv_out[0],
        w_o[0],
        norm_ffn[0][None, :],
        w_up[0],
        ffn_conv_w[0],
        ffn_conv_b[0][None, :],
        w_down[0],
        norm_final[None, :],
    )

    any_spec = pl.BlockSpec(memory_space=pl.ANY)
    in_specs = [any_spec] + [
        any_spec if k in matmul_weights else _whole(op.shape)
        for k, op in enumerate(operands) if k > 0]
    resident_bf16 = [pltpu.VMEM(operands[k].shape, bf16) for k in (2, 5, 7, 8, 10, 13)]

    return pl.pallas_call(
        functools.partial(_block_kernel, tiles_per_row, n_tiles),
        grid=(1,),
        in_specs=in_specs,
        out_specs=any_spec,
        out_shape=jax.ShapeDtypeStruct(x.shape, x.dtype),
        scratch_shapes=[
            pltpu.VMEM((2, ts, D_MODEL), jnp.float32),
            pltpu.VMEM((2, ts, D_MODEL), jnp.float32),
            pltpu.SemaphoreType.DMA((2, 2)),
            pltpu.VMEM((D_MODEL // LANES, POOL_HIST + ts, LANES), jnp.float32),
            pltpu.VMEM((D_MODEL // LANES, CONV_HIST + ts, LANES), jnp.float32),
            pltpu.VMEM((ts, D_MODEL), jnp.float32),
            pltpu.VMEM((ts, D_MODEL), bf16),
        ] + resident_bf16 + [
            pltpu.VMEM((2 * FFN_CHUNK // LANES, CONV_HIST + ts, LANES), jnp.float32)
            for _ in range(N_FFN_CHUNKS)
        ],
        compiler_params=pltpu.CompilerParams(
            dimension_semantics=("arbitrary",),
            vmem_limit_bytes=V7X_VMEM_LIMIT_BYTES,
        ),
        name="hybrid_block",
    )(*operands)
```

```python
import functools

import jax
import jax.numpy as jnp
from jax import lax
from jax.experimental import pallas as pl
from jax.experimental.pallas import tpu as pltpu

D_MODEL = 1024
N_POOL_GROUPS = 4
POOL_WINDOWS = (2, 4, 8, 16)
POOL_GROUP_DIM = D_MODEL // N_POOL_GROUPS
MXU_TILE = 256
D_FF = 2816
KSIZE = 3
RMS_EPS = 1e-6
LANES = 128
BF16_ROWS = 16

SEQ_TILE = 256
POOL_HIST = 16
CONV_HIST = 8
FFN_CHUNK = 256
N_FFN_CHUNKS = D_FF // FFN_CHUNK
UP_AHEAD = 2
CAST_CHUNK_BYTES = 1024 * 1024
CAST_SLOTS = 6
V7X_VMEM_LIMIT_BYTES = 60 * 1024 * 1024


def _rms_norm(x, g):
    inv = lax.rsqrt(jnp.mean(x * x, axis=-1, keepdims=True) + RMS_EPS)
    return x * inv * g


def _dot(a, b):
    return jnp.dot(a, b, preferred_element_type=jnp.float32)


def _slab_store(buf, row0, val):
    for j in range(buf.shape[0]):
        buf[j, row0:row0 + val.shape[0], :] = val[:, j * LANES:(j + 1) * LANES]


def _slab_rows(buf, j, start, rows):
    return buf[pl.ds(j, 1, stride=2), pl.ds(start, rows), :][0]


def _slab_window(buf, start, rows):
    return jnp.concatenate([_slab_rows(buf, j, start, rows) for j in range(buf.shape[0])], axis=-1)


def _carry_history(buf, hist, rows, keep):
    buf[:, 0:hist, :] = jnp.where(keep, buf[:, rows:rows + hist, :], 0.0)


def _causal_taps(buf, hist, rows, w):
    out = None
    for k in range(KSIZE):
        term = w[k:k + 1, :] * _slab_window(buf, hist - (KSIZE - 1) + k, rows)
        out = term if out is None else out + term
    return out


def _block(j):
    return slice(j * MXU_TILE, (j + 1) * MXU_TILE)


def _tile_pair(x_ref, x_next_ref, h_ref, h_next_ref, o_ref, x1_buf, h2_buf, t_tok, t_ch,
               tiles_per_row,
               g_mix_ref, w_in_ref, w_pool_ref, conv_w_ref,
               w_conv_out_ref, w_o_ref, g_ffn_ref, w_up_ref, ffn_conv_w_ref, ffn_conv_b_ref,
               w_down_ref, g_final_ref, zp_buf, cv_buf, u_bufs):
    ts = x_ref.shape[0]
    bf16 = jnp.bfloat16
    tok_continues = t_tok + 1 < tiles_per_row
    ch_continues = t_ch + 1 < tiles_per_row

    def gate_val(ref, c):
        gate = ref[:, c * FFN_CHUNK:(c + 1) * FFN_CHUNK]
        val = ref[:, D_FF + c * FFN_CHUNK:D_FF + (c + 1) * FFN_CHUNK]
        return jnp.concatenate([gate, val], axis=1)

    def up_proj(c):
        _slab_store(u_bufs[c], CONV_HIST, _dot(h2_buf[...], gate_val(w_up_ref, c)))

    def ffn_act(c):
        u_buf = u_bufs[c]
        u = (_causal_taps(u_buf, CONV_HIST, ts, gate_val(ffn_conv_w_ref, c))
             + gate_val(ffn_conv_b_ref, c))
        _carry_history(u_buf, CONV_HIST, ts, ch_continues)
        return (jax.nn.silu(u[:, :FFN_CHUNK]) * u[:, FFN_CHUNK:]).astype(bf16)

    def down_proj(c, act):
        return _dot(act, w_down_ref[c * FFN_CHUNK:(c + 1) * FFN_CHUNK, :])

    def in_proj(seg):
        return _dot(h_ref[...], w_in_ref[:, seg * D_MODEL:(seg + 1) * D_MODEL])

    def blocked_dot(lhs_block, w_ref, blocks):
        out = None
        for j in blocks:
            part = _dot(lhs_block(j), w_ref[_block(j), :])
            out = part if out is None else out + part
        return out

    all_blocks = range(D_MODEL // MXU_TILE)

    def pool_windows():
        pos = t_tok * ts + lax.broadcasted_iota(jnp.int32, (ts, 1), 0) + 1
        slabs_per_group = POOL_GROUP_DIM // LANES
        pooled = []
        for gi, win in enumerate(POOL_WINDOWS):
            def window(back):
                return jnp.concatenate(
                    [_slab_rows(zp_buf, gi * slabs_per_group + j, POOL_HIST - back, ts)
                     for j in range(slabs_per_group)], axis=-1)
            ug = window(0)
            s = ug
            for k in range(1, win):
                s = s + window(k)
            inv_count = 1.0 / jnp.minimum(pos, win).astype(jnp.float32)
            pooled.append((s * inv_count - ug).astype(bf16))
        _carry_history(zp_buf, POOL_HIST, ts, tok_continues)
        return pooled

    v = {}

    def t_conv_c():
        v["z_c"] = in_proj(2)

    def t_windows():
        v["pooled"] = pool_windows()

    def t_conv_v():
        _slab_store(cv_buf, CONV_HIST, v["z_c"] * in_proj(3))

    def t_conv_taps():
        v["conv"] = _causal_taps(cv_buf, CONV_HIST, ts, conv_w_ref[...])
        _carry_history(cv_buf, CONV_HIST, ts, tok_continues)

    def t_conv_b():
        v["conv_in"] = (in_proj(1) * v["conv"]).astype(bf16)

    def t_pool_proj():
        v["y_pool"] = blocked_dot(lambda j: v["pooled"][j], w_pool_ref, all_blocks)

    def t_gate_pool():
        v["gated_pool"] = jax.nn.sigmoid(in_proj(4)) * v["y_pool"]

    def t_gate_conv():
        v["gate_conv"] = jax.nn.sigmoid(in_proj(5))

    def t_conv_out():
        y_conv = blocked_dot(lambda j: v["conv_in"][:, _block(j)], w_conv_out_ref, all_blocks)
        v["merged"] = (v["gated_pool"] + v["gate_conv"] * y_conv).astype(bf16)

    def t_out_proj_a():
        v["x1"] = v["x"] + blocked_dot(lambda j: v["merged"][:, _block(j)], w_o_ref, (0, 1))

    def t_out_proj_b():
        x1 = v["x1"] + blocked_dot(lambda j: v["merged"][:, _block(j)], w_o_ref, (2, 3))
        x1_buf[...] = x1
        h2_buf[...] = _rms_norm(x1, g_ffn_ref[...]).astype(bf16)

    def t_next_norm():
        h_next_ref[...] = _rms_norm(x_next_ref[...], g_mix_ref[...]).astype(bf16)

    def in_pool(h):
        _slab_store(zp_buf, POOL_HIST, _dot(h[...], w_in_ref[:, 0:D_MODEL]))

    def up_head():
        for c in range(UP_AHEAD):
            up_proj(c)

    def token_head():
        v["x"] = x_ref[...]

    def main():
        token_pieces = (t_conv_c, t_windows, t_conv_v, t_conv_taps, t_next_norm, t_conv_b,
                        t_pool_proj, t_gate_pool, t_gate_conv, t_conv_out, t_out_proj_a)
        x2 = x1_buf[...]
        for c in range(N_FFN_CHUNKS):
            act = ffn_act(c)
            if c + UP_AHEAD < N_FFN_CHUNKS:
                up_proj(c + UP_AHEAD)
            if token_pieces[c] is not None:
                token_pieces[c]()
            x2 = x2 + down_proj(c, act)
        v["x2"] = x2

    def tail():
        o_ref[...] = _rms_norm(v["x2"], g_final_ref[...])
        t_out_proj_b()
        in_pool(h_next_ref)

    def pair():
        up_head()
        token_head()
        main()
        tail()

    def token_only():
        in_pool(h_ref)
        token_head()
        for piece in (t_conv_c, t_windows, t_conv_v, t_conv_taps, t_next_norm, t_conv_b,
                      t_pool_proj, t_gate_pool, t_gate_conv, t_conv_out, t_out_proj_a,
                      t_out_proj_b):
            piece()
        in_pool(h_next_ref)

    def channel_only():
        up_head()
        x2 = x1_buf[...]
        for c in range(N_FFN_CHUNKS):
            act = ffn_act(c)
            if c + UP_AHEAD < N_FFN_CHUNKS:
                up_proj(c + UP_AHEAD)
            x2 = x2 + down_proj(c, act)
        o_ref[...] = _rms_norm(x2, g_final_ref[...])

    return pair, token_only, channel_only


def _cast_chunk_rows(rows, cols):
    budget = max(BF16_ROWS, CAST_CHUNK_BYTES // (4 * cols))
    return max(r for r in range(BF16_ROWS, min(rows, budget) + 1, BF16_ROWS) if rows % r == 0)


def _load_as_bf16(src_hbm, dst, chunk_rows):
    rows, cols = src_hbm.shape
    assert rows % chunk_rows == 0 and dst.shape == src_hbm.shape
    n_chunks = rows // chunk_rows
    ahead = CAST_SLOTS - 1

    def scoped(stage, sems):
        def chunk_copy(i, slot):
            return pltpu.make_async_copy(
                src_hbm.at[pl.ds(i * chunk_rows, chunk_rows), :], stage.at[slot], sems.at[slot])

        for k in range(min(ahead, n_chunks)):
            chunk_copy(k, k).start()

        def body(i, carry):
            @pl.when(i + ahead < n_chunks)
            def _():
                chunk_copy(i + ahead, lax.rem(i + ahead, CAST_SLOTS)).start()

            slot = lax.rem(i, CAST_SLOTS)
            chunk_copy(i, slot).wait()
            row0 = pl.multiple_of(i * chunk_rows, chunk_rows)
            dst[pl.ds(row0, chunk_rows), :] = stage[slot].astype(jnp.bfloat16)
            return carry

        lax.fori_loop(0, n_chunks, body, 0)

    pl.run_scoped(scoped, pltpu.VMEM((CAST_SLOTS, chunk_rows, cols), jnp.float32),
                  pltpu.SemaphoreType.DMA((CAST_SLOTS,)))


def _fold_pool_weights(pool_w_hbm, pool_scale_ref, w_pool_proj_hbm, dst):
    def scoped(pool_w, w_pool_proj, sems):
        copies = (pltpu.make_async_copy(pool_w_hbm, pool_w, sems.at[0]),
                  pltpu.make_async_copy(w_pool_proj_hbm, w_pool_proj, sems.at[1]))
        for copy in copies:
            copy.start()
        for copy in copies:
            copy.wait()
        for gi in range(N_POOL_GROUPS):
            scaled = pool_w[_block(gi), :] * pool_scale_ref[:, _block(gi)]
            dst[_block(gi), :] = jnp.dot(
                scaled, w_pool_proj[_block(gi), :], precision=lax.Precision.HIGHEST,
                preferred_element_type=jnp.float32).astype(jnp.bfloat16)

    pl.run_scoped(scoped, pltpu.VMEM(pool_w_hbm.shape, jnp.float32),
                  pltpu.VMEM(w_pool_proj_hbm.shape, jnp.float32), pltpu.SemaphoreType.DMA((2,)))


def _block_kernel(tiles_per_row, n_tiles, x_hbm, *refs):
    (g_mix_ref, w_in_hbm, pool_w_hbm, pool_scale_ref, w_pool_proj_hbm, conv_w_ref,
     w_conv_out_hbm, w_o_hbm, g_ffn_ref, w_up_hbm, ffn_conv_w_ref, ffn_conv_b_ref, w_down_hbm,
     g_final_ref, o_hbm, x_stage, o_stage, x_sems, o_sems, zp_buf, cv_buf, x1_buf, h2_buf,
     h_buf, w_in_ref, w_pool_ref, w_conv_out_ref, w_o_ref, w_up_ref, w_down_ref) = refs[:30]
    u_bufs = refs[30:]
    weights = (g_mix_ref, w_in_ref, w_pool_ref, conv_w_ref,
               w_conv_out_ref, w_o_ref, g_ffn_ref, w_up_ref, ffn_conv_w_ref, ffn_conv_b_ref,
               w_down_ref, g_final_ref)
    ts = x_stage.shape[1]

    def tile_of(ref, tile):
        row0 = pl.multiple_of(lax.rem(tile, tiles_per_row) * ts, ts)
        return ref.at[tile // tiles_per_row, pl.ds(row0, ts), :]

    def x_copy(tile, slot):
        return pltpu.make_async_copy(tile_of(x_hbm, tile), x_stage.at[slot], x_sems.at[slot])

    def o_copy(tile, slot):
        return pltpu.make_async_copy(o_stage.at[slot], tile_of(o_hbm, tile), o_sems.at[slot])

    x_copy(0, 0).start()
    zp_buf[:, 0:POOL_HIST, :] = jnp.zeros((zp_buf.shape[0], POOL_HIST, LANES), jnp.float32)
    for buf in (cv_buf,) + tuple(u_bufs):
        buf[:, 0:CONV_HIST, :] = jnp.zeros((buf.shape[0], CONV_HIST, LANES), jnp.float32)
    for src, dst in ((w_in_hbm, w_in_ref), (w_conv_out_hbm, w_conv_out_ref), (w_o_hbm, w_o_ref),
                     (w_up_hbm, w_up_ref), (w_down_hbm, w_down_ref)):
        _load_as_bf16(src, dst, _cast_chunk_rows(*src.shape))
    _fold_pool_weights(pool_w_hbm, pool_scale_ref, w_pool_proj_hbm, w_pool_ref)

    def phases(i, tok_tile, ch_tile):
        def mod(a, m):
            return a % m if isinstance(a, int) else lax.rem(a, m)

        return _tile_pair(
            x_stage.at[mod(i, 3)], x_stage.at[mod(i + 1, 3)], h_buf.at[mod(i, 2)],
            h_buf.at[mod(i + 1, 2)], o_stage.at[mod(i, 2)], x1_buf, h2_buf,
            lax.rem(tok_tile, tiles_per_row), lax.rem(ch_tile, tiles_per_row), tiles_per_row,
            *weights, zp_buf, cv_buf, u_bufs)

    x_copy(1, 1).start()
    x_copy(2, 2).start()
    x_copy(0, 0).wait()
    h_buf[0] = _rms_norm(x_stage[0], g_mix_ref[...]).astype(jnp.bfloat16)
    x_copy(1, 1).wait()
    _, token_only, _ = phases(0, 0, 0)
    token_only()

    def step(i, carry):
        @pl.when(i + 2 < n_tiles)
        def _():
            x_copy(i + 2, lax.rem(i + 2, 3)).start()

        @pl.when(i + 1 < n_tiles)
        def _():
            x_copy(i + 1, lax.rem(i + 1, 3)).wait()

        slot = lax.rem(i, 2)

        @pl.when(i >= 3)
        def _():
            o_copy(i - 3, slot).wait()

        pair, _, _ = phases(i, i, i - 1)
        pair()
        o_copy(i - 1, slot).start()
        return carry

    lax.fori_loop(1, n_tiles, step, 0)
    last_slot = n_tiles % 2
    o_copy(n_tiles - 3, last_slot).wait()
    _, _, channel_only = phases(n_tiles, n_tiles - 1, n_tiles - 1)
    channel_only()
    o_copy(n_tiles - 1, last_slot).start()
    o_copy(n_tiles - 2, 1 - last_slot).wait()
    o_copy(n_tiles - 1, last_slot).wait()


def _whole(shape):
    return pl.BlockSpec(shape, lambda i: (0,) * len(shape))


@jax.jit
def kernel(x, norm_mix, w_in, pool_w, pool_scale, w_pool_proj, conv_w, w_conv_out, w_o,
           norm_ffn, w_up, ffn_conv_w, ffn_conv_b, w_down, norm_final):
    batch, seq, d_model = x.shape
    ts = SEQ_TILE
    assert d_model == D_MODEL and seq % ts == 0 and D_FF % FFN_CHUNK == 0
    assert POOL_GROUP_DIM == MXU_TILE and N_FFN_CHUNKS == 11
    assert norm_mix.shape[0] == 1, "single-layer block"
    bf16 = jnp.bfloat16
    tiles_per_row = seq // ts
    n_tiles = batch * tiles_per_row
    assert n_tiles >= 3, "the output-copy bookkeeping assumes at least three tiles"

    matmul_weights = {2, 3, 5, 7, 8, 10, 13}
    operands = (
        x,
        norm_mix[0][None, :],
        w_in[0],
        pool_w[0].reshape(N_POOL_GROUPS * POOL_GROUP_DIM, POOL_GROUP_DIM),
        pool_scale[0][None, :],
        w_pool_proj[0],
        conv_w[0],
        w_conv_out[0],
        w_o[0],
        norm_ffn[0][None, :],
        w_up[0],
        ffn_conv_w[0],
        ffn_conv_b[0][None, :],
        w_down[0],
        norm_final[None, :],
    )

    any_spec = pl.BlockSpec(memory_space=pl.ANY)
    in_specs = [any_spec] + [
        any_spec if k in matmul_weights else _whole(op.shape)
        for k, op in enumerate(operands) if k > 0]
    resident_bf16 = [pltpu.VMEM(operands[k].shape, bf16) for k in (2, 5, 7, 8, 10, 13)]

    return pl.pallas_call(
        functools.partial(_block_kernel, tiles_per_row, n_tiles),
        grid=(1,),
        in_specs=in_specs,
        out_specs=any_spec,
        out_shape=jax.ShapeDtypeStruct(x.shape, x.dtype),
        scratch_shapes=[
            pltpu.VMEM((3, ts, D_MODEL), jnp.float32),
            pltpu.VMEM((2, ts, D_MODEL), jnp.float32),
            pltpu.SemaphoreType.DMA((3,)),
            pltpu.SemaphoreType.DMA((2,)),
            pltpu.VMEM((D_MODEL // LANES, POOL_HIST + ts, LANES), jnp.float32),
            pltpu.VMEM((D_MODEL // LANES, CONV_HIST + ts, LANES), jnp.float32),
            pltpu.VMEM((ts, D_MODEL), jnp.float32),
            pltpu.VMEM((ts, D_MODEL), bf16),
            pltpu.VMEM((2, ts, D_MODEL), bf16),
        ] + resident_bf16 + [
            pltpu.VMEM((2 * FFN_CHUNK // LANES, CONV_HIST + ts, LANES), jnp.float32)
            for _ in range(N_FFN_CHUNKS)
        ],
        compiler_params=pltpu.CompilerParams(
            dimension_semantics=("arbitrary",),
            vmem_limit_bytes=V7X_VMEM_LIMIT_BYTES,
        ),
        name="hybrid_block",
    )(*operands)
```

```python
import functools

import jax
import jax.numpy as jnp
from jax import lax
from jax.experimental import pallas as pl
from jax.experimental.pallas import tpu as pltpu

D_MODEL = 1024
N_POOL_GROUPS = 4
POOL_WINDOWS = (2, 4, 8, 16)
POOL_GROUP_DIM = D_MODEL // N_POOL_GROUPS
MXU_TILE = 256
D_FF = 2816
KSIZE = 3
RMS_EPS = 1e-6
LANES = 128
BF16_ROWS = 16

SEQ_TILE = 256
POOL_HIST = 16
CONV_HIST = 8
FFN_CHUNK = 256
N_FFN_CHUNKS = D_FF // FFN_CHUNK
UP_AHEAD = 2
CAST_CHUNK_BYTES = 1024 * 1024
CAST_SLOTS = 6
V7X_VMEM_LIMIT_BYTES = 60 * 1024 * 1024


def _rms_norm(x, g):
    inv = lax.rsqrt(jnp.mean(x * x, axis=-1, keepdims=True) + RMS_EPS)
    return x * inv * g


def _dot(a, b):
    return jnp.dot(a, b, preferred_element_type=jnp.float32)


def _slab_store(buf, row0, val):
    for j in range(buf.shape[0]):
        buf[j, row0:row0 + val.shape[0], :] = val[:, j * LANES:(j + 1) * LANES]


def _slab_rows(buf, j, start, rows):
    return buf[pl.ds(j, 1, stride=2), pl.ds(start, rows), :][0]


def _slab_window(buf, start, rows):
    return jnp.concatenate([_slab_rows(buf, j, start, rows) for j in range(buf.shape[0])], axis=-1)


def _carry_history(buf, hist, rows, keep):
    buf[:, 0:hist, :] = jnp.where(keep, buf[:, rows:rows + hist, :], 0.0)


def _causal_taps(buf, hist, rows, w):
    out = None
    for k in range(KSIZE):
        term = w[k:k + 1, :] * _slab_window(buf, hist - (KSIZE - 1) + k, rows)
        out = term if out is None else out + term
    return out


def _block(j):
    return slice(j * MXU_TILE, (j + 1) * MXU_TILE)


def _tile_pair(x_ref, o_ref, x1_buf, h2_buf, t_tok, t_ch, tiles_per_row,
               g_mix_ref, w_in_ref, w_pool_ref, conv_w_ref,
               w_conv_out_ref, w_o_ref, g_ffn_ref, w_up_ref, ffn_conv_w_ref, ffn_conv_b_ref,
               w_down_ref, g_final_ref, zp_buf, cv_buf, u_bufs):
    ts = x_ref.shape[0]
    bf16 = jnp.bfloat16
    tok_continues = t_tok + 1 < tiles_per_row
    ch_continues = t_ch + 1 < tiles_per_row

    def gate_val(ref, c):
        gate = ref[:, c * FFN_CHUNK:(c + 1) * FFN_CHUNK]
        val = ref[:, D_FF + c * FFN_CHUNK:D_FF + (c + 1) * FFN_CHUNK]
        return jnp.concatenate([gate, val], axis=1)

    def up_proj(c):
        _slab_store(u_bufs[c], CONV_HIST, _dot(h2_buf[...], gate_val(w_up_ref, c)))

    def ffn_act(c):
        u_buf = u_bufs[c]
        u = (_causal_taps(u_buf, CONV_HIST, ts, gate_val(ffn_conv_w_ref, c))
             + gate_val(ffn_conv_b_ref, c))
        _carry_history(u_buf, CONV_HIST, ts, ch_continues)
        return (jax.nn.silu(u[:, :FFN_CHUNK]) * u[:, FFN_CHUNK:]).astype(bf16)

    def down_proj(c, act):
        return _dot(act, w_down_ref[c * FFN_CHUNK:(c + 1) * FFN_CHUNK, :])

    def in_proj(seg):
        return _dot(v["h"], w_in_ref[:, seg * D_MODEL:(seg + 1) * D_MODEL])

    def blocked_dot(lhs_block, w_ref, blocks):
        out = None
        for j in blocks:
            part = _dot(lhs_block(j), w_ref[_block(j), :])
            out = part if out is None else out + part
        return out

    all_blocks = range(D_MODEL // MXU_TILE)

    def pool_windows():
        pos = t_tok * ts + lax.broadcasted_iota(jnp.int32, (ts, 1), 0) + 1
        slabs_per_group = POOL_GROUP_DIM // LANES
        pooled = []
        for gi, win in enumerate(POOL_WINDOWS):
            def window(back):
                return jnp.concatenate(
                    [_slab_rows(zp_buf, gi * slabs_per_group + j, POOL_HIST - back, ts)
                     for j in range(slabs_per_group)], axis=-1)
            ug = window(0)
            s = ug
            for k in range(1, win):
                s = s + window(k)
            inv_count = 1.0 / jnp.minimum(pos, win).astype(jnp.float32)
            pooled.append((s * inv_count - ug).astype(bf16))
        _carry_history(zp_buf, POOL_HIST, ts, tok_continues)
        return pooled

    v = {}

    def t_conv_c():
        v["z_c"] = in_proj(2)

    def t_windows():
        v["pooled"] = pool_windows()

    def t_conv_v():
        _slab_store(cv_buf, CONV_HIST, v["z_c"] * in_proj(3))

    def t_conv_taps():
        v["conv"] = _causal_taps(cv_buf, CONV_HIST, ts, conv_w_ref[...])
        _carry_history(cv_buf, CONV_HIST, ts, tok_continues)

    def t_conv_b():
        v["conv_in"] = (in_proj(1) * v["conv"]).astype(bf16)

    def t_pool_proj():
        v["y_pool"] = blocked_dot(lambda j: v["pooled"][j], w_pool_ref, all_blocks)

    def t_gate_pool():
        v["gated_pool"] = jax.nn.sigmoid(in_proj(4)) * v["y_pool"]

    def t_gate_conv():
        v["gate_conv"] = jax.nn.sigmoid(in_proj(5))

    def t_conv_out():
        y_conv = blocked_dot(lambda j: v["conv_in"][:, _block(j)], w_conv_out_ref, all_blocks)
        v["merged"] = (v["gated_pool"] + v["gate_conv"] * y_conv).astype(bf16)

    def t_out_proj_a():
        v["x1"] = v["x"] + blocked_dot(lambda j: v["merged"][:, _block(j)], w_o_ref, (0, 1))

    def t_out_proj_b():
        x1 = v["x1"] + blocked_dot(lambda j: v["merged"][:, _block(j)], w_o_ref, (2, 3))
        x1_buf[...] = x1
        h2_buf[...] = _rms_norm(x1, g_ffn_ref[...]).astype(bf16)

    def up_head():
        for c in range(UP_AHEAD):
            up_proj(c)

    def token_head():
        v["x"] = x_ref[...]
        v["h"] = _rms_norm(v["x"], g_mix_ref[...]).astype(bf16)
        _slab_store(zp_buf, POOL_HIST, in_proj(0))

    def main():
        token_pieces = (t_conv_c, t_windows, t_conv_v, t_conv_taps, None, t_conv_b,
                        t_pool_proj, t_gate_pool, t_gate_conv, t_conv_out, t_out_proj_a)
        x2 = x1_buf[...]
        for c in range(N_FFN_CHUNKS):
            act = ffn_act(c)
            if c + UP_AHEAD < N_FFN_CHUNKS:
                up_proj(c + UP_AHEAD)
            if token_pieces[c] is not None:
                token_pieces[c]()
            x2 = x2 + down_proj(c, act)
        v["x2"] = x2

    def tail():
        o_ref[...] = _rms_norm(v["x2"], g_final_ref[...])
        t_out_proj_b()

    def pair():
        up_head()
        token_head()
        main()
        tail()

    def token_only():
        token_head()
        for piece in (t_conv_c, t_windows, t_conv_v, t_conv_taps, t_conv_b, t_pool_proj,
                      t_gate_pool, t_gate_conv, t_conv_out, t_out_proj_a, t_out_proj_b):
            piece()

    def channel_only():
        up_head()
        x2 = x1_buf[...]
        for c in range(N_FFN_CHUNKS):
            act = ffn_act(c)
            if c + UP_AHEAD < N_FFN_CHUNKS:
                up_proj(c + UP_AHEAD)
            x2 = x2 + down_proj(c, act)
        o_ref[...] = _rms_norm(x2, g_final_ref[...])

    return pair, token_only, channel_only


def _cast_chunk_rows(rows, cols):
    budget = max(BF16_ROWS, CAST_CHUNK_BYTES // (4 * cols))
    return max(r for r in range(BF16_ROWS, min(rows, budget) + 1, BF16_ROWS) if rows % r == 0)


def _load_as_bf16(src_hbm, dst, chunk_rows):
    rows, cols = src_hbm.shape
    assert rows % chunk_rows == 0 and dst.shape == src_hbm.shape
    n_chunks = rows // chunk_rows
    ahead = CAST_SLOTS - 1

    def scoped(stage, sems):
        def chunk_copy(i, slot):
            return pltpu.make_async_copy(
                src_hbm.at[pl.ds(i * chunk_rows, chunk_rows), :], stage.at[slot], sems.at[slot])

        for k in range(min(ahead, n_chunks)):
            chunk_copy(k, k).start()

        def body(i, carry):
            @pl.when(i + ahead < n_chunks)
            def _():
                chunk_copy(i + ahead, lax.rem(i + ahead, CAST_SLOTS)).start()

            slot = lax.rem(i, CAST_SLOTS)
            chunk_copy(i, slot).wait()
            row0 = pl.multiple_of(i * chunk_rows, chunk_rows)
            dst[pl.ds(row0, chunk_rows), :] = stage[slot].astype(jnp.bfloat16)
            return carry

        lax.fori_loop(0, n_chunks, body, 0)

    pl.run_scoped(scoped, pltpu.VMEM((CAST_SLOTS, chunk_rows, cols), jnp.float32),
                  pltpu.SemaphoreType.DMA((CAST_SLOTS,)))


def _fold_pool_weights(pool_w_hbm, pool_scale_ref, w_pool_proj_hbm, dst):
    def scoped(pool_w, w_pool_proj, sems):
        copies = (pltpu.make_async_copy(pool_w_hbm, pool_w, sems.at[0]),
                  pltpu.make_async_copy(w_pool_proj_hbm, w_pool_proj, sems.at[1]))
        for copy in copies:
            copy.start()
        for copy in copies:
            copy.wait()
        for gi in range(N_POOL_GROUPS):
            scaled = pool_w[_block(gi), :] * pool_scale_ref[:, _block(gi)]
            dst[_block(gi), :] = jnp.dot(
                scaled, w_pool_proj[_block(gi), :], precision=lax.Precision.HIGHEST,
                preferred_element_type=jnp.float32).astype(jnp.bfloat16)

    pl.run_scoped(scoped, pltpu.VMEM(pool_w_hbm.shape, jnp.float32),
                  pltpu.VMEM(w_pool_proj_hbm.shape, jnp.float32), pltpu.SemaphoreType.DMA((2,)))


def _block_kernel(tiles_per_row, n_tiles, x_hbm, *refs):
    (g_mix_ref, w_in_hbm, pool_w_hbm, pool_scale_ref, w_pool_proj_hbm, conv_w_ref,
     w_conv_out_hbm, w_o_hbm, g_ffn_ref, w_up_hbm, ffn_conv_w_ref, ffn_conv_b_ref, w_down_hbm,
     g_final_ref, o_hbm, x_stage, o_stage, io_sems, zp_buf, cv_buf, x1_buf, h2_buf,
     w_in_ref, w_pool_ref, w_conv_out_ref, w_o_ref, w_up_ref, w_down_ref) = refs[:28]
    u_bufs = refs[28:]
    weights = (g_mix_ref, w_in_ref, w_pool_ref, conv_w_ref,
               w_conv_out_ref, w_o_ref, g_ffn_ref, w_up_ref, ffn_conv_w_ref, ffn_conv_b_ref,
               w_down_ref, g_final_ref)
    ts = x_stage.shape[1]

    def tile_of(ref, tile):
        row0 = pl.multiple_of(lax.rem(tile, tiles_per_row) * ts, ts)
        return ref.at[tile // tiles_per_row, pl.ds(row0, ts), :]

    def x_copy(tile, slot):
        return pltpu.make_async_copy(tile_of(x_hbm, tile), x_stage.at[slot], io_sems.at[0, slot])

    def o_copy(tile, slot):
        return pltpu.make_async_copy(o_stage.at[slot], tile_of(o_hbm, tile), io_sems.at[1, slot])

    x_copy(0, 0).start()
    zp_buf[:, 0:POOL_HIST, :] = jnp.zeros((zp_buf.shape[0], POOL_HIST, LANES), jnp.float32)
    for buf in (cv_buf,) + tuple(u_bufs):
        buf[:, 0:CONV_HIST, :] = jnp.zeros((buf.shape[0], CONV_HIST, LANES), jnp.float32)
    for src, dst in ((w_in_hbm, w_in_ref), (w_conv_out_hbm, w_conv_out_ref), (w_o_hbm, w_o_ref),
                     (w_up_hbm, w_up_ref), (w_down_hbm, w_down_ref)):
        _load_as_bf16(src, dst, _cast_chunk_rows(*src.shape))
    _fold_pool_weights(pool_w_hbm, pool_scale_ref, w_pool_proj_hbm, w_pool_ref)

    def phases(slot, tok_tile, ch_tile):
        return _tile_pair(
            x_stage.at[slot], o_stage.at[slot], x1_buf, h2_buf,
            lax.rem(tok_tile, tiles_per_row), lax.rem(ch_tile, tiles_per_row), tiles_per_row,
            *weights, zp_buf, cv_buf, u_bufs)

    x_copy(1, 1).start()
    x_copy(0, 0).wait()
    _, token_only, _ = phases(0, 0, 0)
    token_only()

    def step(i, slot):
        if isinstance(i, int):
            if i + 1 < n_tiles:
                x_copy(i + 1, 1 - slot).start()
        else:
            x_copy(i + 1, 1 - slot).start()

        x_copy(i, slot).wait()

        @pl.when(i >= 3)
        def _():
            o_copy(i - 3, slot).wait()

        pair, _, _ = phases(slot, i, i - 1)
        pair()
        o_copy(i - 1, slot).start()

    def two_steps(k, carry):
        i = 1 + 2 * k
        step(i, 1)
        step(i + 1, 0)
        return carry

    n_pair_steps = n_tiles - 1
    lax.fori_loop(0, n_pair_steps // 2, two_steps, 0)
    if n_pair_steps % 2:
        step(n_tiles - 1, (n_tiles - 1) % 2)
    last_slot = n_tiles % 2
    o_copy(n_tiles - 3, last_slot).wait()
    _, _, channel_only = phases(last_slot, n_tiles - 1, n_tiles - 1)
    channel_only()
    o_copy(n_tiles - 1, last_slot).start()
    o_copy(n_tiles - 2, 1 - last_slot).wait()
    o_copy(n_tiles - 1, last_slot).wait()


def _whole(shape):
    return pl.BlockSpec(shape, lambda i: (0,) * len(shape))


@jax.jit
def kernel(x, norm_mix, w_in, pool_w, pool_scale, w_pool_proj, conv_w, w_conv_out, w_o,
           norm_ffn, w_up, ffn_conv_w, ffn_conv_b, w_down, norm_final):
    batch, seq, d_model = x.shape
    ts = SEQ_TILE
    assert d_model == D_MODEL and seq % ts == 0 and D_FF % FFN_CHUNK == 0
    assert POOL_GROUP_DIM == MXU_TILE and N_FFN_CHUNKS == 11
    assert norm_mix.shape[0] == 1, "single-layer block"
    bf16 = jnp.bfloat16
    tiles_per_row = seq // ts
    n_tiles = batch * tiles_per_row
    assert n_tiles >= 3, "the output-copy bookkeeping assumes at least three tiles"

    matmul_weights = {2, 3, 5, 7, 8, 10, 13}
    operands = (
        x,
        norm_mix[0][None, :],
        w_in[0],
        pool_w[0].reshape(N_POOL_GROUPS * POOL_GROUP_DIM, POOL_GROUP_DIM),
        pool_scale[0][None, :],
        w_pool_proj[0],
        conv_w[0],
        w_conv_out[0],
        w_o[0],
        norm_ffn[0][None, :],
        w_up[0],
        ffn_conv_w[0],
        ffn_conv_b[0][None, :],
        w_down[0],
        norm_final[None, :],
    )

    any_spec = pl.BlockSpec(memory_space=pl.ANY)
    in_specs = [any_spec] + [
        any_spec if k in matmul_weights else _whole(op.shape)
        for k, op in enumerate(operands) if k > 0]
    resident_bf16 = [pltpu.VMEM(operands[k].shape, bf16) for k in (2, 5, 7, 8, 10, 13)]

    return pl.pallas_call(
        functools.partial(_block_kernel, tiles_per_row, n_tiles),
        grid=(1,),
        in_specs=in_specs,
        out_specs=any_spec,
        out_shape=jax.ShapeDtypeStruct(x.shape, x.dtype),
        scratch_shapes=[
            pltpu.VMEM((2, ts, D_MODEL), jnp.float32),
            pltpu.VMEM((2, ts, D_MODEL), jnp.float32),
            pltpu.SemaphoreType.DMA((2, 2)),
            pltpu.VMEM((D_MODEL // LANES, POOL_HIST + ts, LANES), jnp.float32),
            pltpu.VMEM((D_MODEL // LANES, CONV_HIST + ts, LANES), jnp.float32),
            pltpu.VMEM((ts, D_MODEL), jnp.float32),
            pltpu.VMEM((ts, D_MODEL), bf16),
        ] + resident_bf16 + [
            pltpu.VMEM((2 * FFN_CHUNK // LANES, CONV_HIST + ts, LANES), jnp.float32)
            for _ in range(N_FFN_CHUNKS)
        ],
        compiler_params=pltpu.CompilerParams(
            dimension_semantics=("arbitrary",),
            vmem_limit_bytes=V7X_VMEM_LIMIT_BYTES,
        ),
        name="hybrid_block",
    )(*operands)
```
